```python
import math
import jax, jax.numpy as jnp
from jax import lax
import numpy as np

D_MODEL = 1024
BATCH = 2
SEQ = 16384
DEPTH = 4
DEC_BATCH = 16
DEC_SEQ = 16
PAST_LEN = 1024

CHUNK = 64
N_META = 16
N_MIXERS = 3
N_GLA = (DEPTH + 2) // 3
N_RWKV = (DEPTH + 1) // 3
N_SSD = DEPTH // 3
ALPHA = (2.0 * DEPTH) ** 0.25
BETA = (8.0 * DEPTH) ** -0.25
LN_EPS = 1e-5

GLA_HEADS = 4
GLA_KD = D_MODEL // 2
GLA_VD = D_MODEL
GLA_DK = GLA_KD // GLA_HEADS
GLA_DV = GLA_VD // GLA_HEADS
GLA_GATE_RANK = 16
GLA_TAU = 16.0
GLA_IN = 2 * GLA_KD + 2 * GLA_VD + GLA_GATE_RANK

RWKV_HEAD = 64
RWKV_HEADS = D_MODEL // RWKV_HEAD
RWKV_DECAY_LORA = 64
RWKV_A_LORA = 64
RWKV_GATE_LORA = 128
RWKV_LN_EPS = 64e-5

SSD_DI = 2 * D_MODEL
SSD_HEADDIM = 64
SSD_HEADS = SSD_DI // SSD_HEADDIM
SSD_GROUPS = 8
SSD_HPG = SSD_HEADS // SSD_GROUPS
SSD_DSTATE = 128
SSD_CONV = 4
SSD_CONV_DIM = SSD_DI + 2 * SSD_GROUPS * SSD_DSTATE
SSD_IN = SSD_DI + SSD_CONV_DIM + SSD_HEADS

FFN_HIDDEN = ((8 * D_MODEL + 3 * 256 - 1) // (3 * 256)) * 256

kernel_name = 'hybrid_gla_rwkv7_ssd_stream_step'

GLA_NAMES = ('gla_w_in', 'gla_w_gate_up', 'gla_b_gate', 'gla_gn_g', 'gla_gn_b', 'gla_w_out')
RWKV_NAMES = ('rwkv_mu', 'rwkv_w_rkv', 'rwkv_w_o', 'rwkv_w0', 'rwkv_w1', 'rwkv_w2', 'rwkv_a0',
              'rwkv_a1', 'rwkv_a2', 'rwkv_g1', 'rwkv_g2', 'rwkv_k_k', 'rwkv_k_a', 'rwkv_r_k',
              'rwkv_lnx_g', 'rwkv_lnx_b')
SSD_NAMES = ('ssd_w_in', 'ssd_conv_w', 'ssd_conv_b', 'ssd_dt_bias', 'ssd_A_log', 'ssd_D',
             'ssd_norm_g', 'ssd_w_out')


def layer_norm(x, g, b, eps=LN_EPS):
    xf = x.astype(jnp.float32)
    mu = jnp.mean(xf, -1, keepdims=True)
    var = jnp.mean(jnp.square(xf - mu), -1, keepdims=True)
    return ((xf - mu) * lax.rsqrt(var + eps) * g + b).astype(x.dtype)


def chunk_len(T):
    L = min(CHUNK, T)
    assert T % L == 0
    return L


def to_chunks(a, L):
    B, T = a.shape[:2]
    return jnp.moveaxis(a.reshape((B, T // L, L) + a.shape[2:]), 1, 0)


def from_chunks(a):
    a = jnp.moveaxis(a, 0, 1)
    return a.reshape((a.shape[0], a.shape[1] * a.shape[2]) + a.shape[3:])


def swiglu(x, w_up, w_down):
    gate, up = jnp.split(x @ w_up, 2, axis=-1)
    return (jax.nn.silu(gate) * up) @ w_down


def gla_mixer(x, S0, w_in, w_gate_up, b_gate, gn_g, gn_b, w_out):
    f32 = jnp.float32
    Bsz, T, _ = x.shape
    q, k, v, gl, r = jnp.split(x @ w_in, [GLA_KD, 2 * GLA_KD, 2 * GLA_KD + GLA_VD,
                                          2 * GLA_KD + GLA_VD + GLA_GATE_RANK], axis=-1)
    log_a = jax.nn.log_sigmoid((gl @ w_gate_up + b_gate).astype(f32)) / GLA_TAU
    hk = (Bsz, T, GLA_HEADS, GLA_DK)
    hv = (Bsz, T, GLA_HEADS, GLA_DV)
    q = q.astype(f32).reshape(hk) * (GLA_DK ** -0.5)
    k = k.astype(f32).reshape(hk)
    v = v.astype(f32).reshape(hv)
    log_a = log_a.reshape(hk)
    L = chunk_len(T)
    causal = jnp.tril(jnp.ones((L, L), dtype=bool))[None, :, :, None, None]

    def step(S, inp):
        qc, kc, vc, gc = inp
        G = jnp.cumsum(gc, axis=1)
        inter = jnp.einsum('blhd,bhdv->blhv', qc * jnp.exp(G), S)
        decay = jnp.exp(jnp.where(causal, G[:, :, None] - G[:, None], -jnp.inf))
        A = jnp.einsum('blshd,bshd->bhls', qc[:, :, None] * decay, kc)
        intra = jnp.einsum('bhls,bshv->blhv', A, vc)
        G_last = G[:, -1]
        S_new = S * jnp.exp(G_last)[..., None] + jnp.einsum(
            'bshd,bshv->bhdv', kc * jnp.exp(G_last[:, None] - G), vc)
        return S_new, inter + intra

    S_fin, o = lax.scan(step, S0.astype(f32), tuple(to_chunks(t, L) for t in (q, k, v, log_a)))
    o = from_chunks(o)
    mu = jnp.mean(o, -1, keepdims=True)
    var = jnp.mean(jnp.square(o - mu), -1, keepdims=True)
    o = ((o - mu) * lax.rsqrt(var + LN_EPS)).reshape(Bsz, T, GLA_VD) * gn_g + gn_b
    out = (o * jax.nn.silu(r.astype(f32))).astype(x.dtype) @ w_out
    return out, S_fin.astype(S0.dtype)


def rwkv_mixer(x, S0, shift0, mu, w_rkv, w_o, w0, w1, w2, a0, a1, a2, g1, g2, k_k, k_a, r_k,
               lnx_g, lnx_b):
    f32 = jnp.float32
    Bsz, T, D = x.shape
    full = jnp.concatenate([shift0.astype(x.dtype), x], axis=1)
    new_shift = full[:, -1:]
    xx = full[:, :-1] - x
    xs = x[:, :, None, :] + xx[:, :, None, :] * mu
    xr, xw, xk, xv, xa, xg = (xs[:, :, c] for c in range(6))
    r = (xr @ w_rkv[0]).astype(f32)
    k = (xk @ w_rkv[1]).astype(f32)
    v = (xv @ w_rkv[2]).astype(f32)
    w_log = -jax.nn.softplus(-(w0 + jnp.tanh(xw @ w1) @ w2).astype(f32)) - 0.5
    decay = jnp.exp(-jnp.exp(w_log))
    a = jax.nn.sigmoid((a0 + (xa @ a1) @ a2).astype(f32))
    g = (jax.nn.sigmoid(xg @ g1) @ g2).astype(f32)
    hs = (Bsz, T, RWKV_HEADS, RWKV_HEAD)
    r, k, v, decay, a = (t.reshape(hs) for t in (r, k, v, decay, a))
    per_head = (RWKV_HEADS, RWKV_HEAD)
    kk = k * k_k.astype(f32).reshape(per_head)
    kk = kk * lax.rsqrt(jnp.sum(kk * kk, -1, keepdims=True) + 1e-12)
    k = k * (1.0 + (a - 1.0) * k_a.astype(f32).reshape(per_head))

    def step(S, inp):
        r_t, w_t, k_t, v_t, kk_t, b_t = inp
        Sa = jnp.einsum('bhvk,bhk->bhv', S, kk_t)
        S = (S * w_t[:, :, None, :] - Sa[..., None] * b_t[:, :, None, :]
             + v_t[..., None] * k_t[:, :, None, :])
        return S, jnp.einsum('bhvk,bhk->bhv', S, r_t)

    seq = tuple(jnp.moveaxis(t, 1, 0) for t in (r, decay, k, v, kk, kk * a))
    S_fin, y = lax.scan(step, S0.astype(f32), seq)
    y = jnp.moveaxis(y, 0, 1)
    mu_y = jnp.mean(y, -1, keepdims=True)
    var_y = jnp.mean(jnp.square(y - mu_y), -1, keepdims=True)
    y = ((y - mu_y) * lax.rsqrt(var_y + RWKV_LN_EPS)).reshape(Bsz, T, D) * lnx_g + lnx_b
    bonus = jnp.sum(r * k * r_k.astype(f32).reshape(per_head), -1, keepdims=True) * v
    y = y + bonus.reshape(Bsz, T, D)
    out = (y * g).astype(x.dtype) @ w_o
    return out, S_fin.astype(S0.dtype), new_shift.astype(shift0.dtype)


def ssd_mixer(x, S0, conv0, w_in, conv_w, conv_b, dt_bias, A_log, D_skip, norm_g, w_out):
    f32 = jnp.float32
    Bsz, T, _ = x.shape
    z, xBC, dt = jnp.split(x @ w_in, [SSD_DI, SSD_DI + SSD_CONV_DIM], axis=-1)
    full = jnp.concatenate([conv0.astype(x.dtype), xBC], axis=1)
    new_conv = full[:, -(SSD_CONV - 1):]
    conv = lax.conv_general_dilated(full, conv_w[:, None, :].astype(full.dtype), (1,), 'VALID',
                                    dimension_numbers=('NWC', 'WIO', 'NWC'),
                                    feature_group_count=SSD_CONV_DIM) + conv_b
    xBC = jax.nn.silu(conv.astype(f32))
    xh, Bm, Cm = jnp.split(xBC, [SSD_DI, SSD_DI + SSD_GROUPS * SSD_DSTATE], axis=-1)
    xh = xh.reshape(Bsz, T, SSD_HEADS, SSD_HEADDIM)
    Bm = Bm.reshape(Bsz, T, SSD_GROUPS, SSD_DSTATE)
    Cm = Cm.reshape(Bsz, T, SSD_GROUPS, SSD_DSTATE)
    dt = jax.nn.softplus(dt.astype(f32) + dt_bias)
    dtA = dt * (-jnp.exp(A_log.astype(f32)))
    xdt = (xh * dt[..., None]).reshape(Bsz, T, SSD_GROUPS, SSD_HPG, SSD_HEADDIM)
    dtA = dtA.reshape(Bsz, T, SSD_GROUPS, SSD_HPG)
    L = chunk_len(T)
    causal = jnp.tril(jnp.ones((L, L), dtype=bool))[None, :, :, None, None]

    def step(S, inp):
        xc, ac, bc, cc = inp
        cum = jnp.cumsum(ac, axis=1)
        Ldec = jnp.exp(jnp.where(causal, cum[:, :, None] - cum[:, None], -jnp.inf))
        CB = jnp.einsum('blgn,bsgn->blsg', cc, bc)
        y = jnp.einsum('blsgh,bsghp->blghp', CB[..., None] * Ldec, xc)
        y = y + jnp.einsum('blgn,bghpn->blghp', cc, S) * jnp.exp(cum)[..., None]
        last = cum[:, -1]
        S_new = S * jnp.exp(last)[..., None, None] + jnp.einsum(
            'bsgh,bsghp,bsgn->bghpn', jnp.exp(last[:, None] - cum), xc, bc)
        return S_new, y

    S_init = S0.astype(f32).reshape(Bsz, SSD_GROUPS, SSD_HPG, SSD_HEADDIM, SSD_DSTATE)
    S_fin, y = lax.scan(step, S_init, tuple(to_chunks(t, L) for t in (xdt, dtA, Bm, Cm)))
    y = from_chunks(y).reshape(Bsz, T, SSD_HEADS, SSD_HEADDIM) + D_skip.astype(f32)[:, None] * xh
    y = y.reshape(Bsz, T, SSD_DI) * jax.nn.silu(z.astype(f32))
    yg = y.reshape(Bsz, T, SSD_GROUPS, SSD_DI // SSD_GROUPS)
    yg = yg * lax.rsqrt(jnp.mean(jnp.square(yg), -1, keepdims=True) + LN_EPS)
    out = (yg.reshape(Bsz, T, SSD_DI) * norm_g).astype(x.dtype) @ w_out
    S_fin = S_fin.reshape(Bsz, SSD_HEADS, SSD_HEADDIM, SSD_DSTATE)
    return out, S_fin.astype(S0.dtype), new_conv.astype(conv0.dtype)


def trunk(x, gla_S, rwkv_S, rwkv_shift, ssd_S, ssd_conv, P, return_output=True):
    new_gla, new_rwkv, new_shift, new_ssm, new_conv = [], [], [], [], []
    for i in range(DEPTH):
        j, kind = i // N_MIXERS, i % N_MIXERS
        if kind == 0:
            h, s = gla_mixer(x, gla_S[j], *(P[n][j] for n in GLA_NAMES))
            new_gla.append(s)
        elif kind == 1:
            h, s, sh = rwkv_mixer(x, rwkv_S[j], rwkv_shift[j], *(P[n][j] for n in RWKV_NAMES))
            new_rwkv.append(s)
            new_shift.append(sh)
        else:
            h, s, c = ssd_mixer(x, ssd_S[j], ssd_conv[j], *(P[n][j] for n in SSD_NAMES))
            new_ssm.append(s)
            new_conv.append(c)
        if i == DEPTH - 1 and not return_output:
            break
        x = layer_norm(ALPHA * x + h, P['ln_g'][i, 0], P['ln_b'][i, 0])
        x = layer_norm(ALPHA * x + swiglu(x, P['ffn_w_up'][i], P['ffn_w_down'][i]),
                       P['ln_g'][i, 1], P['ln_b'][i, 1])
    return (x, jnp.stack(new_gla), jnp.stack(new_rwkv), jnp.stack(new_shift),
            jnp.stack(new_ssm), jnp.stack(new_conv))


def setup_inputs(seed: int = 0) -> dict:
    key = jax.random.key(seed)
    keys = jax.random.split(key, 64)
    counter = [0]

    def nxt():
        kk = keys[counter[0]]
        counter[0] += 1
        return kk

    def nrm(shape, scale=1.0):
        return jax.random.normal(nxt(), shape, jnp.float32) * scale

    def unif(shape, lo, hi):
        return jax.random.uniform(nxt(), shape, jnp.float32, lo, hi)

    D = D_MODEL
    dt0 = jnp.exp(unif((N_SSD, SSD_HEADS), math.log(1e-3), math.log(1e-1)))
    return {
        'x_prompt': nrm((BATCH, SEQ, D)),
        'x_sample': nrm((DEC_BATCH, DEC_SEQ, D)),
        'state_gla': nrm((N_GLA, DEC_BATCH, GLA_HEADS, GLA_DK, GLA_DV)),
        'state_rwkv': nrm((N_RWKV, DEC_BATCH, RWKV_HEADS, RWKV_HEAD, RWKV_HEAD), 0.5),
        'state_shift': nrm((N_RWKV, DEC_BATCH, 1, D)),
        'state_ssm': nrm((N_SSD, DEC_BATCH, SSD_HEADS, SSD_HEADDIM, SSD_DSTATE), 0.5),
        'state_conv': nrm((N_SSD, DEC_BATCH, SSD_CONV - 1, SSD_CONV_DIM)),
        'meta_tokens': nrm((N_META, D)),
        'gla_w_in': nrm((N_GLA, D, GLA_IN), D ** -0.5),
        'gla_w_gate_up': nrm((N_GLA, GLA_GATE_RANK, GLA_KD), GLA_GATE_RANK ** -0.5),
        'gla_b_gate': nrm((N_GLA, GLA_KD), 0.1),
        'gla_gn_g': 1.0 + nrm((N_GLA, GLA_VD), 0.02),
        'gla_gn_b': nrm((N_GLA, GLA_VD), 0.02),
        'gla_w_out': nrm((N_GLA, GLA_VD, D), BETA * GLA_VD ** -0.5),
        'rwkv_mu': unif((N_RWKV, 6, D), 0.0, 1.0),
        'rwkv_w_rkv': nrm((N_RWKV, 3, D, D), D ** -0.5),
        'rwkv_w_o': nrm((N_RWKV, D, D), BETA * D ** -0.5),
        'rwkv_w0': unif((N_RWKV, D), -6.0, 1.0),
        'rwkv_w1': nrm((N_RWKV, D, RWKV_DECAY_LORA), D ** -0.5),
        'rwkv_w2': nrm((N_RWKV, RWKV_DECAY_LORA, D), 0.1 * RWKV_DECAY_LORA ** -0.5),
        'rwkv_a0': nrm((N_RWKV, D), 0.1),
        'rwkv_a1': nrm((N_RWKV, D, RWKV_A_LORA), D ** -0.5),
        'rwkv_a2': nrm((N_RWKV, RWKV_A_LORA, D), 0.1 * RWKV_A_LORA ** -0.5),
        'rwkv_g1': nrm((N_RWKV, D, RWKV_GATE_LORA), D ** -0.5),
        'rwkv_g2': nrm((N_RWKV, RWKV_GATE_LORA, D), RWKV_GATE_LORA ** -0.5),
        'rwkv_k_k': 0.85 + nrm((N_RWKV, D), 0.02),
        'rwkv_k_a': 1.0 + nrm((N_RWKV, D), 0.02),
        'rwkv_r_k': nrm((N_RWKV, D), 0.1),
        'rwkv_lnx_g': 1.0 + nrm((N_RWKV, D), 0.02),
        'rwkv_lnx_b': nrm((N_RWKV, D), 0.02),
        'ssd_w_in': nrm((N_SSD, D, SSD_IN), D ** -0.5),
        'ssd_conv_w': nrm((N_SSD, SSD_CONV, SSD_CONV_DIM), SSD_CONV ** -0.5),
        'ssd_conv_b': nrm((N_SSD, SSD_CONV_DIM), 0.02),
        'ssd_dt_bias': dt0 + jnp.log(-jnp.expm1(-dt0)),
        'ssd_A_log': jnp.log(unif((N_SSD, SSD_HEADS), 1.0, 16.0)),
        'ssd_D': 1.0 + nrm((N_SSD, SSD_HEADS), 0.02),
        'ssd_norm_g': 1.0 + nrm((N_SSD, SSD_DI), 0.02),
        'ssd_w_out': nrm((N_SSD, SSD_DI, D), BETA * SSD_DI ** -0.5),
        'ffn_w_up': nrm((DEPTH, D, 2 * FFN_HIDDEN), D ** -0.5),
        'ffn_w_down': nrm((DEPTH, FFN_HIDDEN, D), BETA * FFN_HIDDEN ** -0.5),
        'ln_g': 1.0 + nrm((DEPTH, 2, D), 0.02),
        'ln_b': nrm((DEPTH, 2, D), 0.02),
    }


def reference(x_prompt, x_sample, state_gla, state_rwkv, state_shift, state_ssm, state_conv,
              meta_tokens, gla_w_in, gla_w_gate_up, gla_b_gate, gla_gn_g, gla_gn_b, gla_w_out,
              rwkv_mu, rwkv_w_rkv, rwkv_w_o, rwkv_w0, rwkv_w1, rwkv_w2, rwkv_a0, rwkv_a1,
              rwkv_a2, rwkv_g1, rwkv_g2, rwkv_k_k, rwkv_k_a, rwkv_r_k, rwkv_lnx_g, rwkv_lnx_b,
              ssd_w_in, ssd_conv_w, ssd_conv_b, ssd_dt_bias, ssd_A_log, ssd_D, ssd_norm_g,
              ssd_w_out, ffn_w_up, ffn_w_down, ln_g, ln_b):
    P = {
        'gla_w_in': gla_w_in, 'gla_w_gate_up': gla_w_gate_up, 'gla_b_gate': gla_b_gate,
        'gla_gn_g': gla_gn_g, 'gla_gn_b': gla_gn_b, 'gla_w_out': gla_w_out,
        'rwkv_mu': rwkv_mu, 'rwkv_w_rkv': rwkv_w_rkv, 'rwkv_w_o': rwkv_w_o, 'rwkv_w0': rwkv_w0,
        'rwkv_w1': rwkv_w1, 'rwkv_w2': rwkv_w2, 'rwkv_a0': rwkv_a0, 'rwkv_a1': rwkv_a1,
        'rwkv_a2': rwkv_a2, 'rwkv_g1': rwkv_g1, 'rwkv_g2': rwkv_g2, 'rwkv_k_k': rwkv_k_k,
        'rwkv_k_a': rwkv_k_a, 'rwkv_r_k': rwkv_r_k, 'rwkv_lnx_g': rwkv_lnx_g,
        'rwkv_lnx_b': rwkv_lnx_b,
        'ssd_w_in': ssd_w_in, 'ssd_conv_w': ssd_conv_w, 'ssd_conv_b': ssd_conv_b,
        'ssd_dt_bias': ssd_dt_bias, 'ssd_A_log': ssd_A_log, 'ssd_D': ssd_D,
        'ssd_norm_g': ssd_norm_g, 'ssd_w_out': ssd_w_out,
        'ffn_w_up': ffn_w_up, 'ffn_w_down': ffn_w_down, 'ln_g': ln_g, 'ln_b': ln_b,
    }
    Bp = x_prompt.shape[0]
    dtype = x_prompt.dtype
    z_gla = jnp.zeros((N_GLA, Bp, GLA_HEADS, GLA_DK, GLA_DV), dtype)
    z_rwkv = jnp.zeros((N_RWKV, Bp, RWKV_HEADS, RWKV_HEAD, RWKV_HEAD), dtype)
    z_shift = jnp.zeros((N_RWKV, Bp, 1, D_MODEL), dtype)
    z_ssm = jnp.zeros((N_SSD, Bp, SSD_HEADS, SSD_HEADDIM, SSD_DSTATE), dtype)
    z_conv = jnp.zeros((N_SSD, Bp, SSD_CONV - 1, SSD_CONV_DIM), dtype)
    meta = jnp.broadcast_to(meta_tokens.astype(dtype)[None], (Bp, N_META, D_MODEL))
    _, m_gla, m_rwkv, m_shift, m_ssm, m_conv = trunk(
        meta, z_gla, z_rwkv, z_shift, z_ssm, z_conv, P, return_output=False)
    y_prompt, p_gla, p_rwkv, p_shift, p_ssm, p_conv = trunk(
        x_prompt, m_gla, m_rwkv, m_shift, m_ssm, m_conv, P)
    y_sample, s_gla, s_rwkv, s_shift, s_ssm, s_conv = trunk(
        x_sample, state_gla, state_rwkv, state_shift, state_ssm, state_conv, P)
    return (y_prompt, y_sample, p_gla, p_rwkv, p_shift, p_ssm, p_conv,
            s_gla, s_rwkv, s_shift, s_ssm, s_conv)
```

```python
import functools
import math

import numpy as np
import jax
import jax.numpy as jnp
from jax import lax
from jax.experimental import pallas as pl
from jax.experimental.pallas import tpu as pltpu

F32 = jnp.float32
BF16 = jnp.bfloat16

D_MODEL = 1024
DEPTH = 4
CHUNK = 64
N_META = 16
ALPHA = (2.0 * DEPTH) ** 0.25
LN_EPS = 1e-5

GLA_HEADS = 4
GLA_KD = D_MODEL // 2
GLA_VD = D_MODEL
GLA_DK = GLA_KD // GLA_HEADS
GLA_DV = GLA_VD // GLA_HEADS
GLA_GATE_RANK = 16
GLA_TAU = 16.0

RWKV_HEAD = 64
RWKV_HEADS = D_MODEL // RWKV_HEAD
RWKV_LN_EPS = 64e-5
RWKV_LORA_PAD = 128

SSD_DI = 2 * D_MODEL
SSD_HEADDIM = 64
SSD_HEADS = SSD_DI // SSD_HEADDIM
SSD_GROUPS = 8
SSD_HPG = SSD_HEADS // SSD_GROUPS
SSD_DSTATE = 128
SSD_CONV = 4
SSD_GN = SSD_GROUPS * SSD_DSTATE
SSD_CONV_DIM = SSD_DI + 2 * SSD_GN
SSD_GW = SSD_HPG * SSD_HEADDIM

FFN_HIDDEN = ((8 * D_MODEL + 3 * 256 - 1) // (3 * 256)) * 256
FFN_TH = FFN_HIDDEN // 2

LANES = 128
SUBLANES = 8
VMEM_LIMIT = 56 * 1024 * 1024


def _dot(a, b):
    return jnp.dot(a.astype(BF16), b.astype(BF16), preferred_element_type=F32)


def _dot_nt(a, b):
    return lax.dot_general(a.astype(BF16), b.astype(BF16), (((1,), (1,)), ((), ())),
                           preferred_element_type=F32)


def _dot_tn(a, b):
    return lax.dot_general(a.astype(BF16), b.astype(BF16), (((0,), (0,)), ((), ())),
                           preferred_element_type=F32)


def _split2(x):
    hi = x.astype(BF16)
    lo = (x - hi.astype(F32)).astype(BF16)
    return hi, lo


def _split3(x):
    hi = x.astype(BF16)
    r = x - hi.astype(F32)
    mid = r.astype(BF16)
    lo = (r - mid.astype(F32)).astype(BF16)
    return hi, mid, lo


def _dot_sel(w, x):
    hi, mid, lo = _split3(x)
    return (jnp.dot(w, hi, preferred_element_type=F32) + jnp.dot(w, mid, preferred_element_type=F32)
            + jnp.dot(w, lo, preferred_element_type=F32))


def _dot_sel_nt(w, x):
    dn = (((1,), (1,)), ((), ()))
    hi, mid, lo = _split3(x)
    return (lax.dot_general(w, hi, dn, preferred_element_type=F32)
            + lax.dot_general(w, mid, dn, preferred_element_type=F32)
            + lax.dot_general(w, lo, dn, preferred_element_type=F32))


def _dot3(a, b):
    ah, al = _split2(a)
    bh, bl = _split2(b)
    return (jnp.dot(ah, bh, preferred_element_type=F32) + jnp.dot(ah, bl, preferred_element_type=F32)
            + jnp.dot(al, bh, preferred_element_type=F32))


def _seg_sum(x, ones_m, exp_m):
    hi, lo = _split2(x)
    s = jnp.dot(hi, ones_m, preferred_element_type=F32) + jnp.dot(lo, ones_m, preferred_element_type=F32)
    shi, slo = _split2(s)
    return jnp.dot(shi, exp_m, preferred_element_type=F32) + jnp.dot(slo, exp_m, preferred_element_type=F32)


def _layer_norm(x, g, b, eps=LN_EPS):
    mu = jnp.mean(x, -1, keepdims=True)
    xc = x - mu
    var = jnp.mean(xc * xc, -1, keepdims=True)
    return xc * lax.rsqrt(var + eps) * g + b


def _sigmoid(x):
    return 1.0 / (1.0 + jnp.exp(-x))


def _softplus(x):
    return jnp.maximum(x, 0.0) + jnp.log1p(jnp.exp(-jnp.abs(x)))


def _iota2(shape, dim):
    return lax.broadcasted_iota(jnp.int32, shape, dim)


def _tile_tri(rows, L):
    i = np.arange(rows)[:, None]
    j = np.arange(rows)[None, :]
    same = (i // L) == (j // L)
    return (same & (j <= i)).astype(np.float32), same.astype(np.float32)


def _gla_selectors(L):
    i = np.arange(L)[:, None]
    j = np.arange(L)[None, :]
    mats = [j <= i, j > i]
    b = L // 2
    while b >= 1:
        blk = i // b
        odd = (blk % 2) == 1
        mats.append(np.where(odd, (j >= blk * b) & (j <= i), (j > i) & (j <= (blk + 1) * b - 1)))
        b //= 2
    return np.concatenate(mats, 0).astype(np.float32)


def _group_mats(width, group):
    i = np.arange(width)[:, None]
    j = np.arange(LANES)[None, :]
    ones = ((i // group) == j).astype(np.float32)
    return ones, ones.T.copy()


def _const(a):
    return jnp.asarray(a, BF16)


def _ffn_kernel(x_ref, wup_ref, wdn_ref, g_ref, b_ref, o_ref):
    x = x_ref[...]
    xb = x.astype(BF16)
    acc = jnp.zeros(x.shape, F32)
    for j in range(FFN_HIDDEN // FFN_TH):
        gate = jnp.dot(xb, wup_ref[:, j * FFN_TH:(j + 1) * FFN_TH], preferred_element_type=F32)
        up = jnp.dot(xb, wup_ref[:, FFN_HIDDEN + j * FFN_TH:FFN_HIDDEN + (j + 1) * FFN_TH],
                     preferred_element_type=F32)
        act = (gate * _sigmoid(gate) * up).astype(BF16)
        acc = acc + jnp.dot(act, wdn_ref[j * FFN_TH:(j + 1) * FFN_TH, :], preferred_element_type=F32)
    o_ref[...] = _layer_norm(ALPHA * x + acc, g_ref[...], b_ref[...])


def _resident(shape):
    nd = len(shape)
    return pl.BlockSpec(shape, lambda *_: (0,) * nd, pipeline_mode=pl.Buffered(1))


def _ffn_layer(x, w_up, w_down, ln_g, ln_b, tm):
    M = x.shape[0]
    return pl.pallas_call(
        _ffn_kernel,
        grid=(M // tm,),
        in_specs=[pl.BlockSpec((tm, D_MODEL), lambda i: (i, 0)),
                  _resident(w_up.shape), _resident(w_down.shape),
                  _resident(ln_g.shape), _resident(ln_b.shape)],
        out_specs=pl.BlockSpec((tm, D_MODEL), lambda i: (i, 0)),
        out_shape=jax.ShapeDtypeStruct((M, D_MODEL), F32),
        compiler_params=pltpu.CompilerParams(dimension_semantics=("arbitrary",),
                                             vmem_limit_bytes=VMEM_LIMIT),
        name="ffn_deepnorm",
    )(x, w_up, w_down, ln_g, ln_b)


def _gla_kernel(x_ref, s0_ref, wqkvr_ref, wgl_ref, wgu_ref, bg_ref, gng_ref, gnb_ref, wo_ref,
                lng_ref, lnb_ref, sel_ref, ones_ref, exp_ref,
                y_ref, s_ref, q_s, k_s, v_s, g_s, o_s, *, bb, tm, L):
    R = bb * tm
    cpt = tm // L
    nlev = int(math.log2(L))

    @pl.when(pl.program_id(1) == 0)
    def _():
        s_ref[...] = s0_ref[...]

    x = x_ref[...].reshape(R, D_MODEL)
    xb = x.astype(BF16)
    proj = jnp.dot(xb, wqkvr_ref[...], preferred_element_type=F32)
    q_s[...] = proj[:, :GLA_KD] * (GLA_DK ** -0.5)
    k_s[...] = proj[:, GLA_KD:2 * GLA_KD]
    v_s[...] = proj[:, 2 * GLA_KD:2 * GLA_KD + GLA_VD]
    r = proj[:, 2 * GLA_KD + GLA_VD:]
    gl = jnp.dot(xb, wgl_ref[...], preferred_element_type=F32)
    pre = _dot(gl, wgu_ref[...]) + bg_ref[...]
    g_s[...] = -_softplus(-pre) / GLA_TAU

    row = _iota2((L, L), 0)
    col = _iota2((L, L), 1)
    eye = row == col
    rowv = _iota2((L, 1), 0)
    pair_masks = []
    odd_rows = []
    for li in range(nlev):
        b = L >> (li + 1)
        pair_masks.append((row ^ col) < 2 * b)
        odd_rows.append((rowv & b) != 0)
    sel = sel_ref[...]

    def chunk(n, carry):
        b_idx = n // cpt
        rows = pl.ds(pl.multiple_of(n * L, L), L)
        q = q_s[rows, :]
        k = k_s[rows, :]
        v = v_s[rows, :]
        g = g_s[rows, :]
        cs = _dot_sel(sel, g)
        G = cs[0:L]
        e_last = jnp.exp(G[L - 1:L])
        qg = q * jnp.exp(G)
        kr = k * jnp.exp(cs[L:2 * L])
        qt, kt = [], []
        for li in range(nlev):
            f = jnp.exp(cs[(2 + li) * L:(3 + li) * L])
            qt.append(jnp.where(odd_rows[li], q * f, 0.0).astype(BF16))
            kt.append(jnp.where(odd_rows[li], 0.0, k * f).astype(BF16))
        for h in range(GLA_HEADS):
            ks = slice(h * GLA_DK, (h + 1) * GLA_DK)
            vs = slice(h * GLA_DV, (h + 1) * GLA_DV)
            A = jnp.where(eye, jnp.sum(q[:, ks] * k[:, ks], -1, keepdims=True), 0.0)
            for li in range(nlev):
                A = A + jnp.where(pair_masks[li], _dot_nt(qt[li][:, ks], kt[li][:, ks]), 0.0)
            st = s_ref[b_idx, h]
            vh = v[:, vs]
            o_s[rows, vs] = _dot_nt(qg[:, ks], st) + _dot(A, vh)
            s_ref[b_idx, h] = st * e_last[:, ks] + _dot_tn(vh, kr[:, ks])
        return carry

    lax.fori_loop(0, bb * cpt, chunk, 0)

    o = o_s[...]
    mu = _seg_sum(o, ones_ref[...], exp_ref[...]) * (1.0 / GLA_DV)
    oc = o - mu
    var = _seg_sum(oc * oc, ones_ref[...], exp_ref[...]) * (1.0 / GLA_DV)
    on = oc * lax.rsqrt(var + LN_EPS) * gng_ref[...] + gnb_ref[...]
    h_mix = _dot(on * (r * _sigmoid(r)), wo_ref[...])
    y_ref[...] = _layer_norm(ALPHA * x + h_mix, lng_ref[...], lnb_ref[...]).reshape(bb, tm, D_MODEL)


def _mixer_call(kernel, name, x, states, consts, out_state_shapes, scratch, bb, tm):
    B, T, _ = x.shape
    grid = (B // bb, T // tm)

    def state_spec(shape):
        nd = len(shape)
        return pl.BlockSpec((bb,) + tuple(shape[1:]), lambda b, t: (b,) + (0,) * (nd - 1))

    in_specs = ([pl.BlockSpec((bb, tm, D_MODEL), lambda b, t: (b, t, 0))]
                + [state_spec(s.shape) for s in states]
                + [_resident(c.shape) for c in consts])
    out_specs = ([pl.BlockSpec((bb, tm, D_MODEL), lambda b, t: (b, t, 0))]
                 + [state_spec(s) for s in out_state_shapes])
    out_shape = ([jax.ShapeDtypeStruct((B, T, D_MODEL), F32)]
                 + [jax.ShapeDtypeStruct(s, F32) for s in out_state_shapes])
    return pl.pallas_call(
        kernel, grid=grid, in_specs=in_specs, out_specs=out_specs, out_shape=out_shape,
        scratch_shapes=scratch,
        compiler_params=pltpu.CompilerParams(dimension_semantics=("arbitrary", "arbitrary"),
                                             vmem_limit_bytes=VMEM_LIMIT),
        name=name,
    )(x, *states, *consts)


def _gla_layer(x, s0, w, ln_g, ln_b, bb, tm, L):
    R = bb * tm
    ones_m, exp_m = _group_mats(GLA_VD, GLA_DV)
    consts = [w["wqkvr"], w["wgl"], w["wgu"], w["bg"], w["gng"], w["gnb"], w["wo"], ln_g, ln_b,
              _const(_gla_selectors(L)), _const(ones_m), _const(exp_m)]
    scratch = [pltpu.VMEM((R, GLA_KD), F32), pltpu.VMEM((R, GLA_KD), F32), pltpu.VMEM((R, GLA_VD), F32),
               pltpu.VMEM((R, GLA_KD), F32), pltpu.VMEM((R, GLA_VD), F32)]
    kern = functools.partial(_gla_kernel, bb=bb, tm=tm, L=L)
    return _mixer_call(kern, "gla_mixer", x, [s0], consts, [s0.shape], scratch, bb, tm)


def _tri_inverse(T, L):
    row = _iota2((L, L), 0)
    col = _iota2((L, L), 1)
    eye = (row == col).astype(F32)
    bs = min(16, L)
    Tb = jnp.where((row ^ col) < bs, T, 0.0)
    X = eye - Tb
    P = Tb
    p = 2
    while p < bs:
        P = _dot3(P, P)
        X = X + _dot3(X, P)
        p *= 2
    s = bs
    while s < L:
        C = jnp.where(((row ^ col) < 2 * s) & ((row ^ col) >= s), T, 0.0)
        X = X - _dot3(_dot3(X, C), X)
        s *= 2
    return X


def _rwkv_kernel(x_ref, s0_ref, sh0_ref, mu_ref, wr_ref, wk_ref, wv_ref, wo_ref, w0_ref, w1_ref, w2_ref,
                 a0_ref, a1_ref, a2_ref, g1_ref, g2_ref, kk_ref, ka_ref, rk_ref, lxg_ref, lxb_ref,
                 lng_ref, lnb_ref, tri_ref, tot_ref, ones_ref, exp_ref,
                 y_ref, s_ref, sh_ref, xs_s, ar_s, bk_s, vv_s, kbl_s, yh_s, *, bb, tm, L):
    R = bb * tm
    cpt = tm // L
    H, N = RWKV_HEADS, RWKV_HEAD

    @pl.when(pl.program_id(1) == 0)
    def _():
        s_ref[...] = s0_ref[...]
        xs_s[:, SUBLANES - 1:SUBLANES, :] = sh0_ref[...]

    x3 = x_ref[...]
    xs_s[:, SUBLANES:SUBLANES + tm, :] = x3
    xprev = xs_s[:, SUBLANES - 1:SUBLANES - 1 + tm, :].reshape(R, D_MODEL)
    last = x3[:, tm - 1:tm, :]
    xs_s[:, SUBLANES - 1:SUBLANES, :] = last
    sh_ref[...] = last
    x = x3.reshape(R, D_MODEL)
    xx = xprev - x
    mu = mu_ref[...]

    def mix(c):
        return (x + xx * mu[c:c + 1, :]).astype(BF16)

    r = jnp.dot(mix(0), wr_ref[...], preferred_element_type=F32)
    k = jnp.dot(mix(2), wk_ref[...], preferred_element_type=F32)
    v = jnp.dot(mix(3), wv_ref[...], preferred_element_type=F32)
    wl = jnp.tanh(jnp.dot(mix(1), w1_ref[...], preferred_element_type=F32))
    w_log = -_softplus(-(w0_ref[...] + _dot(wl, w2_ref[...]))) - 0.5
    lw = -jnp.exp(w_log)
    al = jnp.dot(mix(4), a1_ref[...], preferred_element_type=F32)
    a = _sigmoid(a0_ref[...] + _dot(al, a2_ref[...]))
    gg = _dot(_sigmoid(jnp.dot(mix(5), g1_ref[...], preferred_element_type=F32)), g2_ref[...])

    ones_m = ones_ref[...]
    exp_m = exp_ref[...]
    kk = k * kk_ref[...]
    kk = kk * lax.rsqrt(_seg_sum(kk * kk, ones_m, exp_m) + 1e-12)
    k = k * (1.0 + (a - 1.0) * ka_ref[...])
    bvec = kk * a

    gam = _dot_sel(tri_ref[...], lw)
    gl = _dot_sel(tot_ref[...], lw)
    e_neg = jnp.exp(-gam)
    e_rem = jnp.exp(gl - gam)
    alpha = kk * jnp.exp(gam - lw)
    rho = r * jnp.exp(gam)
    beta = bvec * e_neg
    kap = k * e_neg
    kap_l = k * e_rem
    bet_l = -(bvec * e_rem)
    e_tot = jnp.exp(gl)
    for h in range(H):
        hs = slice(h * N, (h + 1) * N)
        ar_s[h, 0] = alpha[:, hs]
        ar_s[h, 1] = rho[:, hs]
        bk_s[h, 0] = beta[:, hs]
        bk_s[h, 1] = kap[:, hs]
        vv_s[h, 0] = v[:, hs]
        kbl_s[h, 0] = kap_l[:, hs]
        kbl_s[h, 1] = bet_l[:, hs]
        kbl_s[h, 2] = e_tot[:, hs]

    row = _iota2((L, L), 0)
    col = _iota2((L, L), 1)
    strict = col < row
    incl = col <= row

    def head_chunk(idx, carry):
        n = idx // H
        h = idx % H
        b_idx = n // cpt
        rows = pl.ds(pl.multiple_of(n * L, L), L)
        AR = jnp.concatenate([ar_s[h, 0, rows, :], ar_s[h, 1, rows, :]], 0)
        BK = jnp.concatenate([bk_s[h, 0, rows, :], bk_s[h, 1, rows, :]], 0)
        vh = vv_s[h, 0, rows, :]
        S = s_ref[b_idx, h]
        M = _dot_nt(AR, BK)
        Tb = jnp.where(strict, M[:L, :L], 0.0)
        Tk = jnp.where(strict, M[:L, L:], 0.0)
        Rb = jnp.where(incl, M[L:, :L], 0.0)
        Rk = jnp.where(incl, M[L:, L:], 0.0)
        W0 = _dot_nt(AR, S)
        U = _dot3(_tri_inverse(Tb, L), W0[:L] + _dot(Tk, vh))
        yh_s[h, 0, rows, :] = W0[L:] - _dot(Rb, U) + _dot(Rk, vh)
        VU = jnp.concatenate([vh, U], 0)
        KB = jnp.concatenate([kbl_s[h, 0, rows, :], kbl_s[h, 1, rows, :]], 0)
        e_row = kbl_s[h, 2, pl.ds(pl.multiple_of(n * L, L), 1), :]
        s_ref[b_idx, h] = S * e_row + _dot_tn(VU, KB)
        return carry

    lax.fori_loop(0, bb * cpt * H, head_chunk, 0)

    y = jnp.concatenate([yh_s[h, 0] for h in range(H)], -1)
    mu_y = _seg_sum(y, ones_m, exp_m) * (1.0 / N)
    yc = y - mu_y
    var_y = _seg_sum(yc * yc, ones_m, exp_m) * (1.0 / N)
    yn = yc * lax.rsqrt(var_y + RWKV_LN_EPS) * lxg_ref[...] + lxb_ref[...]
    bonus = _seg_sum(r * k * rk_ref[...], ones_m, exp_m) * v
    h_mix = _dot((yn + bonus) * gg, wo_ref[...])
    y_ref[...] = _layer_norm(ALPHA * x + h_mix, lng_ref[...], lnb_ref[...]).reshape(bb, tm, D_MODEL)


def _rwkv_layer(x, s0, sh0, w, ln_g, ln_b, bb, tm, L):
    R = bb * tm
    H, N = RWKV_HEADS, RWKV_HEAD
    tri, tot = _tile_tri(R, L)
    ones_m, exp_m = _group_mats(D_MODEL, N)
    consts = [w["mu"], w["wr"], w["wk"], w["wv"], w["wo"], w["w0"], w["w1"], w["w2"], w["a0"], w["a1"],
              w["a2"], w["g1"], w["g2"], w["kk"], w["ka"], w["rk"], w["lxg"], w["lxb"], ln_g, ln_b,
              _const(tri), _const(tot), _const(ones_m), _const(exp_m)]
    scratch = [pltpu.VMEM((bb, tm + SUBLANES, D_MODEL), F32),
               pltpu.VMEM((H, 2, R, N), F32), pltpu.VMEM((H, 2, R, N), F32), pltpu.VMEM((H, 1, R, N), F32),
               pltpu.VMEM((H, 3, R, N), F32), pltpu.VMEM((H, 1, R, N), F32)]
    kern = functools.partial(_rwkv_kernel, bb=bb, tm=tm, L=L)
    return _mixer_call(kern, "rwkv_mixer", x, [s0, sh0], consts, [s0.shape, sh0.shape], scratch, bb, tm)


def _ssd_kernel(x_ref, s0_ref, cv0_ref, wz_ref, wxbc_ref, wdtx_ref, wdts_ref, cw_ref, cb_ref,
                dbx_ref, dbs_ref, alx_ref, als_ref, dsk_ref, ng_ref, wo_ref, lng_ref, lnb_ref,
                tri_ref, eye_ref, ones_ref, exp_ref,
                y_ref, s_ref, cv_ref, xs_s, xh_s, xdt_s, b_s, c_s, ax_s, as_s, yo_s, *, bb, tm, L):
    R = bb * tm
    cpt = tm // L
    W = SSD_CONV
    PAD = SUBLANES

    @pl.when(pl.program_id(1) == 0)
    def _():
        s_ref[...] = s0_ref[...]
        xs_s[:, PAD - (W - 1):PAD, :] = cv0_ref[...]

    x = x_ref[...].reshape(R, D_MODEL)
    xb = x.astype(BF16)
    xbc_raw = jnp.dot(xb, wxbc_ref[...], preferred_element_type=F32)
    xs_s[:, PAD:PAD + tm, :] = xbc_raw.reshape(bb, tm, SSD_CONV_DIM)
    cw = cw_ref[...]
    conv = cb_ref[...].reshape(1, 1, SSD_CONV_DIM)
    for wi in range(W):
        lo = PAD - (W - 1) + wi
        conv = conv + xs_s[:, lo:lo + tm, :] * cw[wi:wi + 1, :].reshape(1, 1, SSD_CONV_DIM)
    tail = xs_s[:, PAD + tm - (W - 1):PAD + tm, :]
    xs_s[:, PAD - (W - 1):PAD, :] = tail
    cv_ref[...] = tail
    conv = conv.reshape(R, SSD_CONV_DIM)
    xbc = conv * _sigmoid(conv)
    xh = xbc[:, :SSD_DI]
    dt_x = _softplus(jnp.dot(xb, wdtx_ref[...], preferred_element_type=F32) + dbx_ref[...])
    dt_s = _softplus(jnp.dot(xb, wdts_ref[...], preferred_element_type=F32) + dbs_ref[...])
    xh_s[...] = xh
    xdt_s[...] = xh * dt_x
    b_s[...] = xbc[:, SSD_DI:SSD_DI + SSD_GN]
    c_s[...] = xbc[:, SSD_DI + SSD_GN:]
    ax_s[...] = dt_x * (-jnp.exp(alx_ref[...]))
    as_s[...] = dt_s * (-jnp.exp(als_ref[...]))

    row = _iota2((L, L), 0)
    col = _iota2((L, L), 1)
    causal = col <= row
    tri = tri_ref[...]
    eye = eye_ref[...]

    def chunk(n, carry):
        b_idx = n // cpt
        rows = pl.ds(pl.multiple_of(n * L, L), L)
        cum = _dot_sel(tri, ax_s[rows, :])
        cum_t = _dot_sel_nt(eye, _dot_sel(tri, as_s[rows, :]))
        last = cum[L - 1:L]
        e_cum = jnp.exp(cum)
        e_last = jnp.exp(last)
        xdt = xdt_s[rows, :]
        xsc = xdt * jnp.exp(last - cum)
        Bm = b_s[rows, :]
        Cm = c_s[rows, :]
        for g in range(SSD_GROUPS):
            ns = slice(g * SSD_DSTATE, (g + 1) * SSD_DSTATE)
            gs = slice(g * SSD_GW, (g + 1) * SSD_GW)
            Bg = Bm[:, ns]
            Cg = Cm[:, ns]
            CB = _dot_nt(Cg, Bg)
            st = s_ref[b_idx, g]
            ys = []
            for hh in range(SSD_HPG):
                h = g * SSD_HPG + hh
                d = cum[:, h * SSD_HEADDIM:h * SSD_HEADDIM + L] - cum_t[h:h + 1, :]
                dec = jnp.exp(jnp.where(causal, d, -jnp.inf))
                ys.append(_dot(CB * dec, xdt[:, h * SSD_HEADDIM:(h + 1) * SSD_HEADDIM]))
            yo_s[rows, gs] = jnp.concatenate(ys, -1) + _dot(Cg, st) * e_cum[:, gs]
            s_ref[b_idx, g] = st * e_last[:, gs] + _dot_tn(Bg, xsc[:, gs])
        return carry

    lax.fori_loop(0, bb * cpt, chunk, 0)

    z = jnp.dot(xb, wz_ref[...], preferred_element_type=F32)
    y = (yo_s[...] + dsk_ref[...] * xh_s[...]) * (z * _sigmoid(z))
    ms = _seg_sum(y * y, ones_ref[...], exp_ref[...]) * (1.0 / SSD_GW)
    yn = y * lax.rsqrt(ms + LN_EPS) * ng_ref[...]
    h_mix = _dot(yn, wo_ref[...])
    y_ref[...] = _layer_norm(ALPHA * x + h_mix, lng_ref[...], lnb_ref[...]).reshape(bb, tm, D_MODEL)


def _ssd_layer(x, s0, cv0, w, ln_g, ln_b, bb, tm, L):
    R = bb * tm
    tri, _ = _tile_tri(L, L)
    ones_m, exp_m = _group_mats(SSD_DI, SSD_GW)
    consts = [w["wz"], w["wxbc"], w["wdtx"], w["wdts"], w["cw"], w["cb"], w["dbx"], w["dbs"], w["alx"],
              w["als"], w["dsk"], w["ng"], w["wo"], ln_g, ln_b,
              _const(tri), _const(np.eye(LANES, dtype=np.float32)), _const(ones_m), _const(exp_m)]
    scratch = [pltpu.VMEM((bb, tm + SUBLANES, SSD_CONV_DIM), F32),
               pltpu.VMEM((R, SSD_DI), F32), pltpu.VMEM((R, SSD_DI), F32),
               pltpu.VMEM((R, SSD_GN), F32), pltpu.VMEM((R, SSD_GN), F32),
               pltpu.VMEM((R, SSD_DI), F32), pltpu.VMEM((R, LANES), F32), pltpu.VMEM((R, SSD_DI), F32)]
    kern = functools.partial(_ssd_kernel, bb=bb, tm=tm, L=L)
    return _mixer_call(kern, "ssd_mixer", x, [s0, cv0], consts, [s0.shape, cv0.shape], scratch, bb, tm)


def _row(v):
    return v.reshape(1, -1).astype(F32)


def _pad_cols(w, n):
    return jnp.pad(w, ((0, 0), (0, n - w.shape[1])))


def _pad_rows(w, n):
    return jnp.pad(w, ((0, n - w.shape[0]), (0, 0)))


def _prep_gla(p, j):
    w_in = p["gla_w_in"][j]
    o_gl = 2 * GLA_KD + GLA_VD
    return {
        "wqkvr": jnp.concatenate([w_in[:, :o_gl], w_in[:, o_gl + GLA_GATE_RANK:]], 1).astype(BF16),
        "wgl": _pad_cols(w_in[:, o_gl:o_gl + GLA_GATE_RANK], LANES).astype(BF16),
        "wgu": _pad_rows(p["gla_w_gate_up"][j], LANES).astype(BF16),
        "bg": _row(p["gla_b_gate"][j]), "gng": _row(p["gla_gn_g"][j]), "gnb": _row(p["gla_gn_b"][j]),
        "wo": p["gla_w_out"][j].astype(BF16),
    }


def _prep_rwkv(p, j):
    return {
        "mu": p["rwkv_mu"][j].astype(F32),
        "wr": p["rwkv_w_rkv"][j, 0].astype(BF16), "wk": p["rwkv_w_rkv"][j, 1].astype(BF16),
        "wv": p["rwkv_w_rkv"][j, 2].astype(BF16), "wo": p["rwkv_w_o"][j].astype(BF16),
        "w0": _row(p["rwkv_w0"][j]),
        "w1": _pad_cols(p["rwkv_w1"][j], RWKV_LORA_PAD).astype(BF16),
        "w2": _pad_rows(p["rwkv_w2"][j], RWKV_LORA_PAD).astype(BF16),
        "a0": _row(p["rwkv_a0"][j]),
        "a1": _pad_cols(p["rwkv_a1"][j], RWKV_LORA_PAD).astype(BF16),
        "a2": _pad_rows(p["rwkv_a2"][j], RWKV_LORA_PAD).astype(BF16),
        "g1": p["rwkv_g1"][j].astype(BF16), "g2": p["rwkv_g2"][j].astype(BF16),
        "kk": _row(p["rwkv_k_k"][j]), "ka": _row(p["rwkv_k_a"][j]), "rk": _row(p["rwkv_r_k"][j]),
        "lxg": _row(p["rwkv_lnx_g"][j]), "lxb": _row(p["rwkv_lnx_b"][j]),
    }


def _prep_ssd(p, j):
    w_in = p["ssd_w_in"][j]
    w_dt = w_in[:, SSD_DI + SSD_CONV_DIM:]
    rep = lambda v: jnp.repeat(v, SSD_HEADDIM, axis=-1)
    return {
        "wz": w_in[:, :SSD_DI].astype(BF16),
        "wxbc": w_in[:, SSD_DI:SSD_DI + SSD_CONV_DIM].astype(BF16),
        "wdtx": rep(w_dt).astype(BF16),
        "wdts": _pad_cols(w_dt, LANES).astype(BF16),
        "cw": p["ssd_conv_w"][j].astype(F32), "cb": _row(p["ssd_conv_b"][j]),
        "dbx": _row(rep(p["ssd_dt_bias"][j])), "dbs": _pad_cols(_row(p["ssd_dt_bias"][j]), LANES),
        "alx": _row(rep(p["ssd_A_log"][j])), "als": _pad_cols(_row(p["ssd_A_log"][j]), LANES),
        "dsk": _row(rep(p["ssd_D"][j])), "ng": _row(p["ssd_norm_g"][j]),
        "wo": p["ssd_w_out"][j].astype(BF16),
    }


def _trunk(x, st, W, bb, tm, L, ffn_tm):
    B, T, _ = x.shape
    new = {"gla": [], "rwkv": [], "shift": [], "ssm": [], "conv": []}
    for i in range(DEPTH):
        j, kind = i // 3, i % 3
        lg, lb = W["ln_g"][i], W["ln_b"][i]
        if kind == 0:
            x, s = _gla_layer(x, st["gla"][j], W["gla"][j], lg[0], lb[0], bb, tm, L)
            new["gla"].append(s)
        elif kind == 1:
            x, s, sh = _rwkv_layer(x, st["rwkv"][j], st["shift"][j], W["rwkv"][j], lg[0], lb[0], bb, tm, L)
            new["rwkv"].append(s)
            new["shift"].append(sh)
        else:
            x, s, c = _ssd_layer(x, st["ssm"][j], st["conv"][j], W["ssd"][j], lg[0], lb[0], bb, tm, L)
            new["ssm"].append(s)
            new["conv"].append(c)
        x = _ffn_layer(x.reshape(B * T, D_MODEL), W["ffn_up"][i], W["ffn_down"][i], lg[1], lb[1],
                       ffn_tm).reshape(B, T, D_MODEL)
    return x, new


def _ssm_to_kernel(s):
    B = s.shape[0]
    return jnp.swapaxes(s.reshape(B, SSD_GROUPS, SSD_GW, SSD_DSTATE), -1, -2)


def _ssm_from_kernel(s):
    B = s.shape[0]
    return jnp.swapaxes(s, -1, -2).reshape(B, SSD_HEADS, SSD_HEADDIM, SSD_DSTATE)


def kernel(x_prompt, x_sample, state_gla, state_rwkv, state_shift, state_ssm, state_conv, meta_tokens, gla_w_in, gla_w_gate_up, gla_b_gate, gla_gn_g, gla_gn_b, gla_w_out, rwkv_mu, rwkv_w_rkv, rwkv_w_o, rwkv_w0, rwkv_w1, rwkv_w2, rwkv_a0, rwkv_a1, rwkv_a2, rwkv_g1, rwkv_g2, rwkv_k_k, rwkv_k_a, rwkv_r_k, rwkv_lnx_g, rwkv_lnx_b, ssd_w_in, ssd_conv_w, ssd_conv_b, ssd_dt_bias, ssd_A_log, ssd_D, ssd_norm_g, ssd_w_out, ffn_w_up, ffn_w_down, ln_g, ln_b):
    p = dict(gla_w_in=gla_w_in, gla_w_gate_up=gla_w_gate_up, gla_b_gate=gla_b_gate, gla_gn_g=gla_gn_g,
             gla_gn_b=gla_gn_b, gla_w_out=gla_w_out, rwkv_mu=rwkv_mu, rwkv_w_rkv=rwkv_w_rkv,
             rwkv_w_o=rwkv_w_o, rwkv_w0=rwkv_w0, rwkv_w1=rwkv_w1, rwkv_w2=rwkv_w2, rwkv_a0=rwkv_a0,
             rwkv_a1=rwkv_a1, rwkv_a2=rwkv_a2, rwkv_g1=rwkv_g1, rwkv_g2=rwkv_g2, rwkv_k_k=rwkv_k_k,
             rwkv_k_a=rwkv_k_a, rwkv_r_k=rwkv_r_k, rwkv_lnx_g=rwkv_lnx_g, rwkv_lnx_b=rwkv_lnx_b,
             ssd_w_in=ssd_w_in, ssd_conv_w=ssd_conv_w, ssd_conv_b=ssd_conv_b, ssd_dt_bias=ssd_dt_bias,
             ssd_A_log=ssd_A_log, ssd_D=ssd_D, ssd_norm_g=ssd_norm_g, ssd_w_out=ssd_w_out)
    n_gla, n_rwkv, n_ssd = gla_w_in.shape[0], rwkv_mu.shape[0], ssd_w_in.shape[0]
    W = {
        "gla": [_prep_gla(p, j) for j in range(n_gla)],
        "rwkv": [_prep_rwkv(p, j) for j in range(n_rwkv)],
        "ssd": [_prep_ssd(p, j) for j in range(n_ssd)],
        "ffn_up": [ffn_w_up[i].astype(BF16) for i in range(DEPTH)],
        "ffn_down": [ffn_w_down[i].astype(BF16) for i in range(DEPTH)],
        "ln_g": [[_row(ln_g[i, c]) for c in range(2)] for i in range(DEPTH)],
        "ln_b": [[_row(ln_b[i, c]) for c in range(2)] for i in range(DEPTH)],
    }
    Bp, Tp, _ = x_prompt.shape
    Bs, Ts, _ = x_sample.shape
    assert Ts == N_META and Tp % CHUNK == 0

    def with_meta(s):
        return jnp.concatenate([jnp.zeros((Bp,) + s.shape[1:], F32), s.astype(F32)], 0)

    x_small = jnp.concatenate(
        [jnp.broadcast_to(meta_tokens.astype(F32)[None], (Bp, N_META, D_MODEL)), x_sample.astype(F32)], 0)
    st_small = {
        "gla": [with_meta(jnp.swapaxes(state_gla[j], -1, -2)) for j in range(n_gla)],
        "rwkv": [with_meta(state_rwkv[j]) for j in range(n_rwkv)],
        "shift": [with_meta(state_shift[j]) for j in range(n_rwkv)],
        "ssm": [with_meta(_ssm_to_kernel(state_ssm[j])) for j in range(n_ssd)],
        "conv": [with_meta(state_conv[j]) for j in range(n_ssd)],
    }
    Bsm = Bp + Bs
    bb_small = 3 if Bsm % 3 == 0 else 1
    y_small, new_small = _trunk(x_small, st_small, W, bb_small, N_META, N_META, Bsm * N_META)
    st_prompt = {k: [s[:Bp] for s in v] for k, v in new_small.items()}
    tm = 256 if Tp % 256 == 0 else CHUNK
    ffn_tm = 512 if (Bp * Tp) % 512 == 0 else CHUNK
    y_prompt, new_prompt = _trunk(x_prompt.astype(F32), st_prompt, W, 1, tm, CHUNK, ffn_tm)

    def outs(new, sl):
        return (jnp.stack([jnp.swapaxes(s[sl], -1, -2) for s in new["gla"]]),
                jnp.stack([s[sl] for s in new["rwkv"]]),
                jnp.stack([s[sl] for s in new["shift"]]),
                jnp.stack([_ssm_from_kernel(s[sl]) for s in new["ssm"]]),
                jnp.stack([s[sl] for s in new["conv"]]))

    return (y_prompt, y_small[Bp:]) + outs(new_prompt, slice(None)) + outs(new_small, slice(Bp, None))
```

```python
import functools
import math

import numpy as np
import jax
import jax.numpy as jnp
from jax import lax
from jax.experimental import pallas as pl
from jax.experimental.pallas import tpu as pltpu

F32 = jnp.float32
BF16 = jnp.bfloat16

D_MODEL = 1024
DEPTH = 4
CHUNK = 64
N_META = 16
ALPHA = (2.0 * DEPTH) ** 0.25
LN_EPS = 1e-5

GLA_HEADS = 4
GLA_KD = D_MODEL // 2
GLA_VD = D_MODEL
GLA_DK = GLA_KD // GLA_HEADS
GLA_DV = GLA_VD // GLA_HEADS
GLA_GATE_RANK = 16
GLA_TAU = 16.0

RWKV_HEAD = 64
RWKV_HEADS = D_MODEL // RWKV_HEAD
RWKV_LN_EPS = 64e-5
RWKV_LORA_PAD = 128
RWKV_HEAD_GROUP = 8

SSD_DI = 2 * D_MODEL
SSD_HEADDIM = 64
SSD_HEADS = SSD_DI // SSD_HEADDIM
SSD_GROUPS = 8
SSD_HPG = SSD_HEADS // SSD_GROUPS
SSD_DSTATE = 128
SSD_CONV = 4
SSD_GN = SSD_GROUPS * SSD_DSTATE
SSD_CONV_DIM = SSD_DI + 2 * SSD_GN
SSD_GW = SSD_HPG * SSD_HEADDIM

FFN_HIDDEN = ((8 * D_MODEL + 3 * 256 - 1) // (3 * 256)) * 256
FFN_TH = FFN_HIDDEN // 2

LANES = 128
SUBLANES = 8
VMEM_LIMIT = 56 * 1024 * 1024


def _dot(a, b):
    return jnp.dot(a.astype(BF16), b.astype(BF16), preferred_element_type=F32)


def _dot_nt(a, b):
    return lax.dot_general(a.astype(BF16), b.astype(BF16), (((1,), (1,)), ((), ())),
                           preferred_element_type=F32)


def _dot_tn(a, b):
    return lax.dot_general(a.astype(BF16), b.astype(BF16), (((0,), (0,)), ((), ())),
                           preferred_element_type=F32)


def _split2(x):
    hi = x.astype(BF16)
    lo = (x - hi.astype(F32)).astype(BF16)
    return hi, lo


def _split3(x):
    hi = x.astype(BF16)
    r = x - hi.astype(F32)
    mid = r.astype(BF16)
    lo = (r - mid.astype(F32)).astype(BF16)
    return hi, mid, lo


def _dot_sel(w, x):
    hi, mid, lo = _split3(x)
    return (jnp.dot(w, hi, preferred_element_type=F32) + jnp.dot(w, mid, preferred_element_type=F32)
            + jnp.dot(w, lo, preferred_element_type=F32))


def _dot_sel_r(x, w):
    hi, mid, lo = _split3(x)
    return (jnp.dot(hi, w, preferred_element_type=F32) + jnp.dot(mid, w, preferred_element_type=F32)
            + jnp.dot(lo, w, preferred_element_type=F32))


def _dot_sel_nt(w, x):
    dn = (((1,), (1,)), ((), ()))
    hi, mid, lo = _split3(x)
    return (lax.dot_general(w, hi, dn, preferred_element_type=F32)
            + lax.dot_general(w, mid, dn, preferred_element_type=F32)
            + lax.dot_general(w, lo, dn, preferred_element_type=F32))


def _dot3(a, b):
    ah, al = _split2(a)
    bh, bl = _split2(b)
    return (jnp.dot(ah, bh, preferred_element_type=F32) + jnp.dot(ah, bl, preferred_element_type=F32)
            + jnp.dot(al, bh, preferred_element_type=F32))


def _seg_sum(x, ones_m, exp_m):
    hi, lo = _split2(x)
    s = jnp.dot(hi, ones_m, preferred_element_type=F32) + jnp.dot(lo, ones_m, preferred_element_type=F32)
    shi, slo = _split2(s)
    return jnp.dot(shi, exp_m, preferred_element_type=F32) + jnp.dot(slo, exp_m, preferred_element_type=F32)


def _layer_norm(x, g, b, eps=LN_EPS):
    mu = jnp.mean(x, -1, keepdims=True)
    xc = x - mu
    var = jnp.mean(xc * xc, -1, keepdims=True)
    return xc * lax.rsqrt(var + eps) * g + b


def _sigmoid(x):
    return 1.0 / (1.0 + jnp.exp(-x))


def _softplus(x):
    return jnp.maximum(x, 0.0) + jnp.log1p(jnp.exp(-jnp.abs(x)))


def _iota2(shape, dim):
    return lax.broadcasted_iota(jnp.int32, shape, dim)


def _tile_tri(rows, L):
    i = np.arange(rows)[:, None]
    j = np.arange(rows)[None, :]
    same = (i // L) == (j // L)
    return (same & (j <= i)).astype(np.float32), same.astype(np.float32)


def _gla_selectors(L):
    i = np.arange(L)[:, None]
    j = np.arange(L)[None, :]
    mats = [j <= i, j > i]
    b = L // 2
    while b >= 1:
        blk = i // b
        odd = (blk % 2) == 1
        mats.append(np.where(odd, (j >= blk * b) & (j <= i), (j > i) & (j <= (blk + 1) * b - 1)))
        b //= 2
    return np.concatenate(mats, 0).astype(np.float32)


def _group_mats(width, group):
    i = np.arange(width)[:, None]
    j = np.arange(LANES)[None, :]
    ones = ((i // group) == j).astype(np.float32)
    return ones, ones.T.copy()


def _const(a):
    return jnp.asarray(a, BF16)


def _ffn_kernel(x_ref, wup_ref, wdn_ref, g_ref, b_ref, o_ref):
    x = x_ref[...]
    xb = x.astype(BF16)
    acc = jnp.zeros(x.shape, F32)
    for j in range(FFN_HIDDEN // FFN_TH):
        gate = jnp.dot(xb, wup_ref[:, j * FFN_TH:(j + 1) * FFN_TH], preferred_element_type=F32)
        up = jnp.dot(xb, wup_ref[:, FFN_HIDDEN + j * FFN_TH:FFN_HIDDEN + (j + 1) * FFN_TH],
                     preferred_element_type=F32)
        act = (gate * _sigmoid(gate) * up).astype(BF16)
        acc = acc + jnp.dot(act, wdn_ref[j * FFN_TH:(j + 1) * FFN_TH, :], preferred_element_type=F32)
    o_ref[...] = _layer_norm(ALPHA * x + acc, g_ref[...], b_ref[...])


def _resident(shape):
    nd = len(shape)
    return pl.BlockSpec(shape, lambda *_: (0,) * nd, pipeline_mode=pl.Buffered(1))


def _ffn_layer(x, w_up, w_down, ln_g, ln_b, tm):
    M = x.shape[0]
    return pl.pallas_call(
        _ffn_kernel,
        grid=(M // tm,),
        in_specs=[pl.BlockSpec((tm, D_MODEL), lambda i: (i, 0)),
                  _resident(w_up.shape), _resident(w_down.shape),
                  _resident(ln_g.shape), _resident(ln_b.shape)],
        out_specs=pl.BlockSpec((tm, D_MODEL), lambda i: (i, 0)),
        out_shape=jax.ShapeDtypeStruct((M, D_MODEL), F32),
        compiler_params=pltpu.CompilerParams(dimension_semantics=("arbitrary",),
                                             vmem_limit_bytes=VMEM_LIMIT),
        name="ffn_deepnorm",
    )(x, w_up, w_down, ln_g, ln_b)


def _gla_kernel(x_ref, s0_ref, wqkvr_ref, wgl_ref, wgu_ref, bg_ref, gng_ref, gnb_ref, wo_ref,
                lng_ref, lnb_ref, sel_ref, ones_ref, exp_ref,
                y_ref, s_ref, q_s, k_s, v_s, g_s, o_s, *, bb, tm, L):
    R = bb * tm
    cpt = tm // L
    nlev = int(math.log2(L))

    @pl.when(pl.program_id(1) == 0)
    def _():
        s_ref[...] = s0_ref[...]

    x = x_ref[...].reshape(R, D_MODEL)
    xb = x.astype(BF16)
    proj = jnp.dot(xb, wqkvr_ref[...], preferred_element_type=F32)
    q_s[...] = proj[:, :GLA_KD] * (GLA_DK ** -0.5)
    k_s[...] = proj[:, GLA_KD:2 * GLA_KD]
    v_s[...] = proj[:, 2 * GLA_KD:2 * GLA_KD + GLA_VD]
    r = proj[:, 2 * GLA_KD + GLA_VD:]
    gl = jnp.dot(xb, wgl_ref[...], preferred_element_type=F32)
    pre = _dot(gl, wgu_ref[...]) + bg_ref[...]
    g_s[...] = -_softplus(-pre) / GLA_TAU

    row = _iota2((L, L), 0)
    col = _iota2((L, L), 1)
    eye = row == col
    rowv = _iota2((L, 1), 0)
    pair_masks = []
    odd_rows = []
    for li in range(nlev):
        b = L >> (li + 1)
        pair_masks.append((row ^ col) < 2 * b)
        odd_rows.append((rowv & b) != 0)
    sel = sel_ref[...]

    def chunk(n, carry):
        b_idx = n // cpt
        rows = pl.ds(pl.multiple_of(n * L, L), L)
        q = q_s[rows, :]
        k = k_s[rows, :]
        v = v_s[rows, :]
        g = g_s[rows, :]
        cs = _dot_sel(sel, g)
        G = cs[0:L]
        e_last = jnp.exp(G[L - 1:L])
        qg = q * jnp.exp(G)
        kr = k * jnp.exp(cs[L:2 * L])
        qt, kt = [], []
        for li in range(nlev):
            f = jnp.exp(cs[(2 + li) * L:(3 + li) * L])
            qt.append(jnp.where(odd_rows[li], q * f, 0.0).astype(BF16))
            kt.append(jnp.where(odd_rows[li], 0.0, k * f).astype(BF16))
        heads = range(GLA_HEADS)
        ks = [slice(h * GLA_DK, (h + 1) * GLA_DK) for h in heads]
        vs = [slice(h * GLA_DV, (h + 1) * GLA_DV) for h in heads]
        sts = [s_ref[b_idx, h] for h in heads]
        scores = [[_dot_nt(qt[li][:, ks[h]], kt[li][:, ks[h]]) for li in range(nlev)] for h in heads]
        inter = [_dot_nt(qg[:, ks[h]], sts[h]) for h in heads]
        upd = [_dot_tn(v[:, vs[h]], kr[:, ks[h]]) for h in heads]
        for h in heads:
            A = jnp.where(eye, jnp.sum(q[:, ks[h]] * k[:, ks[h]], -1, keepdims=True), 0.0)
            for li in range(nlev):
                A = A + jnp.where(pair_masks[li], scores[h][li], 0.0)
            o_s[rows, vs[h]] = inter[h] + _dot(A, v[:, vs[h]])
            s_ref[b_idx, h] = sts[h] * e_last[:, ks[h]] + upd[h]
        return carry

    lax.fori_loop(0, bb * cpt, chunk, 0)

    o = o_s[...]
    mu = _seg_sum(o, ones_ref[...], exp_ref[...]) * (1.0 / GLA_DV)
    oc = o - mu
    var = _seg_sum(oc * oc, ones_ref[...], exp_ref[...]) * (1.0 / GLA_DV)
    on = oc * lax.rsqrt(var + LN_EPS) * gng_ref[...] + gnb_ref[...]
    h_mix = _dot(on * (r * _sigmoid(r)), wo_ref[...])
    y_ref[...] = _layer_norm(ALPHA * x + h_mix, lng_ref[...], lnb_ref[...]).reshape(bb, tm, D_MODEL)


def _mixer_call(kernel, name, x, states, consts, out_state_shapes, scratch, bb, tm):
    B, T, _ = x.shape
    grid = (B // bb, T // tm)

    def state_spec(shape):
        nd = len(shape)
        return pl.BlockSpec((bb,) + tuple(shape[1:]), lambda b, t: (b,) + (0,) * (nd - 1))

    in_specs = ([pl.BlockSpec((bb, tm, D_MODEL), lambda b, t: (b, t, 0))]
                + [state_spec(s.shape) for s in states]
                + [_resident(c.shape) for c in consts])
    out_specs = ([pl.BlockSpec((bb, tm, D_MODEL), lambda b, t: (b, t, 0))]
                 + [state_spec(s) for s in out_state_shapes])
    out_shape = ([jax.ShapeDtypeStruct((B, T, D_MODEL), F32)]
                 + [jax.ShapeDtypeStruct(s, F32) for s in out_state_shapes])
    return pl.pallas_call(
        kernel, grid=grid, in_specs=in_specs, out_specs=out_specs, out_shape=out_shape,
        scratch_shapes=scratch,
        compiler_params=pltpu.CompilerParams(dimension_semantics=("arbitrary", "arbitrary"),
                                             vmem_limit_bytes=VMEM_LIMIT),
        name=name,
    )(x, *states, *consts)


def _gla_layer(x, s0, w, ln_g, ln_b, bb, tm, L):
    R = bb * tm
    ones_m, exp_m = _group_mats(GLA_VD, GLA_DV)
    consts = [w["wqkvr"], w["wgl"], w["wgu"], w["bg"], w["gng"], w["gnb"], w["wo"], ln_g, ln_b,
              _const(_gla_selectors(L)), _const(ones_m), _const(exp_m)]
    scratch = [pltpu.VMEM((R, GLA_KD), F32), pltpu.VMEM((R, GLA_KD), F32), pltpu.VMEM((R, GLA_VD), F32),
               pltpu.VMEM((R, GLA_KD), F32), pltpu.VMEM((R, GLA_VD), F32)]
    kern = functools.partial(_gla_kernel, bb=bb, tm=tm, L=L)
    return _mixer_call(kern, "gla_mixer", x, [s0], consts, [s0.shape], scratch, bb, tm)


def _tri_inverse(Ts, L):
    row = _iota2((L, L), 0)
    col = _iota2((L, L), 1)
    eye = (row == col).astype(F32)
    bs = min(16, L)
    Ps = [jnp.where((row ^ col) < bs, T, 0.0) for T in Ts]
    Xs = [eye - P for P in Ps]
    p = 2
    while p < bs:
        Ps = [_dot3(P, P) for P in Ps]
        Xs = [X + _dot3(X, P) for X, P in zip(Xs, Ps)]
        p *= 2
    s = bs
    while s < L:
        off = ((row ^ col) < 2 * s) & ((row ^ col) >= s)
        XC = [_dot3(X, jnp.where(off, T, 0.0)) for X, T in zip(Xs, Ts)]
        Xs = [X - _dot3(Y, X) for X, Y in zip(Xs, XC)]
        s *= 2
    return Xs


def _rwkv_kernel(x_ref, s0_ref, sh0_ref, mu_ref, wr_ref, wk_ref, wv_ref, wo_ref, w0_ref, w1_ref, w2_ref,
                 a0_ref, a1_ref, a2_ref, g1_ref, g2_ref, kk_ref, ka_ref, rk_ref, lxg_ref, lxb_ref,
                 lng_ref, lnb_ref, tri_ref, tot_ref, ones_ref, exp_ref,
                 y_ref, s_ref, sh_ref, xs_s, ar_s, bk_s, vv_s, kbl_s, yh_s, *, bb, tm, L):
    R = bb * tm
    cpt = tm // L
    H, N = RWKV_HEADS, RWKV_HEAD

    @pl.when(pl.program_id(1) == 0)
    def _():
        s_ref[...] = s0_ref[...]
        xs_s[:, SUBLANES - 1:SUBLANES, :] = sh0_ref[...]

    x3 = x_ref[...]
    xs_s[:, SUBLANES:SUBLANES + tm, :] = x3
    xprev = xs_s[:, SUBLANES - 1:SUBLANES - 1 + tm, :].reshape(R, D_MODEL)
    last = x3[:, tm - 1:tm, :]
    xs_s[:, SUBLANES - 1:SUBLANES, :] = last
    sh_ref[...] = last
    x = x3.reshape(R, D_MODEL)
    xx = xprev - x
    mu = mu_ref[...]

    def mix(c):
        return (x + xx * mu[c:c + 1, :]).astype(BF16)

    r = jnp.dot(mix(0), wr_ref[...], preferred_element_type=F32)
    k = jnp.dot(mix(2), wk_ref[...], preferred_element_type=F32)
    v = jnp.dot(mix(3), wv_ref[...], preferred_element_type=F32)
    wl = jnp.tanh(jnp.dot(mix(1), w1_ref[...], preferred_element_type=F32))
    w_log = -_softplus(-(w0_ref[...] + _dot(wl, w2_ref[...]))) - 0.5
    lw = -jnp.exp(w_log)
    al = jnp.dot(mix(4), a1_ref[...], preferred_element_type=F32)
    a = _sigmoid(a0_ref[...] + _dot(al, a2_ref[...]))
    gg = _dot(_sigmoid(jnp.dot(mix(5), g1_ref[...], preferred_element_type=F32)), g2_ref[...])

    ones_m = ones_ref[...]
    exp_m = exp_ref[...]
    kk = k * kk_ref[...]
    kk = kk * lax.rsqrt(_seg_sum(kk * kk, ones_m, exp_m) + 1e-12)
    k = k * (1.0 + (a - 1.0) * ka_ref[...])
    bvec = kk * a

    gam = _dot_sel(tri_ref[...], lw)
    gl = _dot_sel(tot_ref[...], lw)
    e_neg = jnp.exp(-gam)
    e_rem = jnp.exp(gl - gam)
    alpha = kk * jnp.exp(gam - lw)
    rho = r * jnp.exp(gam)
    beta = bvec * e_neg
    kap = k * e_neg
    kap_l = k * e_rem
    bet_l = -(bvec * e_rem)
    e_tot = jnp.exp(gl)
    for h in range(H):
        hs = slice(h * N, (h + 1) * N)
        ar_s[h, 0] = alpha[:, hs]
        ar_s[h, 1] = rho[:, hs]
        bk_s[h, 0] = beta[:, hs]
        bk_s[h, 1] = kap[:, hs]
        vv_s[h, 0] = v[:, hs]
        kbl_s[h, 0] = kap_l[:, hs]
        kbl_s[h, 1] = bet_l[:, hs]
        kbl_s[h, 2] = e_tot[:, hs]

    row = _iota2((L, L), 0)
    col = _iota2((L, L), 1)
    strict = col < row
    incl = col <= row

    def head_chunks(ops):
        Ms = [_dot_nt(AR, BK) for AR, BK, _, _, _, _ in ops]
        W0s = [_dot_nt(AR, S) for AR, _, _, _, _, S in ops]
        Xs = _tri_inverse([jnp.where(strict, M[:L, :L], 0.0) for M in Ms], L)
        Ws = [W0[:L] + _dot(jnp.where(strict, M[:L, L:], 0.0), o[2]) for W0, M, o in zip(W0s, Ms, ops)]
        Us = [_dot3(X, Wm) for X, Wm in zip(Xs, Ws)]
        ys = [W0[L:] - _dot(jnp.where(incl, M[L:, :L], 0.0), U) + _dot(jnp.where(incl, M[L:, L:], 0.0), o[2])
              for W0, M, U, o in zip(W0s, Ms, Us, ops)]
        Ss = [o[5] * o[4] + _dot_tn(jnp.concatenate([o[2], U], 0), o[3]) for U, o in zip(Us, ops)]
        return ys, Ss

    n_groups = H // RWKV_HEAD_GROUP

    def head_group(idx, carry):
        n = idx // n_groups
        h0 = (idx % n_groups) * RWKV_HEAD_GROUP
        b_idx = n // cpt
        rows = pl.ds(pl.multiple_of(n * L, L), L)
        row0 = pl.ds(pl.multiple_of(n * L, L), 1)
        ops = []
        for j in range(RWKV_HEAD_GROUP):
            h = h0 + j
            ops.append((jnp.concatenate([ar_s[h, 0, rows, :], ar_s[h, 1, rows, :]], 0),
                        jnp.concatenate([bk_s[h, 0, rows, :], bk_s[h, 1, rows, :]], 0),
                        vv_s[h, 0, rows, :],
                        jnp.concatenate([kbl_s[h, 0, rows, :], kbl_s[h, 1, rows, :]], 0),
                        kbl_s[h, 2, row0, :],
                        s_ref[b_idx, h]))
        ys, Ss = head_chunks(ops)
        for j in range(RWKV_HEAD_GROUP):
            yh_s[h0 + j, 0, rows, :] = ys[j]
            s_ref[b_idx, h0 + j] = Ss[j]
        return carry

    lax.fori_loop(0, bb * cpt * n_groups, head_group, 0)

    y = jnp.concatenate([yh_s[h, 0] for h in range(H)], -1)
    mu_y = _seg_sum(y, ones_m, exp_m) * (1.0 / N)
    yc = y - mu_y
    var_y = _seg_sum(yc * yc, ones_m, exp_m) * (1.0 / N)
    yn = yc * lax.rsqrt(var_y + RWKV_LN_EPS) * lxg_ref[...] + lxb_ref[...]
    bonus = _seg_sum(r * k * rk_ref[...], ones_m, exp_m) * v
    h_mix = _dot((yn + bonus) * gg, wo_ref[...])
    y_ref[...] = _layer_norm(ALPHA * x + h_mix, lng_ref[...], lnb_ref[...]).reshape(bb, tm, D_MODEL)


def _rwkv_layer(x, s0, sh0, w, ln_g, ln_b, bb, tm, L):
    R = bb * tm
    H, N = RWKV_HEADS, RWKV_HEAD
    tri, tot = _tile_tri(R, L)
    ones_m, exp_m = _group_mats(D_MODEL, N)
    consts = [w["mu"], w["wr"], w["wk"], w["wv"], w["wo"], w["w0"], w["w1"], w["w2"], w["a0"], w["a1"],
              w["a2"], w["g1"], w["g2"], w["kk"], w["ka"], w["rk"], w["lxg"], w["lxb"], ln_g, ln_b,
              _const(tri), _const(tot), _const(ones_m), _const(exp_m)]
    scratch = [pltpu.VMEM((bb, tm + SUBLANES, D_MODEL), F32),
               pltpu.VMEM((H, 2, R, N), F32), pltpu.VMEM((H, 2, R, N), F32), pltpu.VMEM((H, 1, R, N), F32),
               pltpu.VMEM((H, 3, R, N), F32), pltpu.VMEM((H, 1, R, N), F32)]
    kern = functools.partial(_rwkv_kernel, bb=bb, tm=tm, L=L)
    return _mixer_call(kern, "rwkv_mixer", x, [s0, sh0], consts, [s0.shape, sh0.shape], scratch, bb, tm)


def _ssd_kernel(x_ref, s0_ref, cv0_ref, wz_ref, wxbc_ref, wdts_ref, cw_ref, cb_ref,
                dbs_ref, alx_ref, als_ref, dsk_ref, ng_ref, wo_ref, lng_ref, lnb_ref,
                tri_ref, eye_ref, hexp_ref, ones_ref, exp_ref,
                y_ref, s_ref, cv_ref, xs_s, xh_s, xdt_s, b_s, c_s, ax_s, as_s, yo_s, *, bb, tm, L):
    R = bb * tm
    cpt = tm // L
    W = SSD_CONV
    PAD = SUBLANES

    @pl.when(pl.program_id(1) == 0)
    def _():
        s_ref[...] = s0_ref[...]
        xs_s[:, PAD - (W - 1):PAD, :] = cv0_ref[...]

    x = x_ref[...].reshape(R, D_MODEL)
    xb = x.astype(BF16)
    xbc_raw = jnp.dot(xb, wxbc_ref[...], preferred_element_type=F32)
    xs_s[:, PAD:PAD + tm, :] = xbc_raw.reshape(bb, tm, SSD_CONV_DIM)
    cw = cw_ref[...]
    conv = cb_ref[...].reshape(1, 1, SSD_CONV_DIM)
    for wi in range(W):
        lo = PAD - (W - 1) + wi
        conv = conv + xs_s[:, lo:lo + tm, :] * cw[wi:wi + 1, :].reshape(1, 1, SSD_CONV_DIM)
    tail = xs_s[:, PAD + tm - (W - 1):PAD + tm, :]
    xs_s[:, PAD - (W - 1):PAD, :] = tail
    cv_ref[...] = tail
    conv = conv.reshape(R, SSD_CONV_DIM)
    xbc = conv * _sigmoid(conv)
    xh = xbc[:, :SSD_DI]
    dt_s = _softplus(jnp.dot(xb, wdts_ref[...], preferred_element_type=F32) + dbs_ref[...])
    dt_x = _dot_sel_r(dt_s, hexp_ref[...])
    xh_s[...] = xh
    xdt_s[...] = xh * dt_x
    b_s[...] = xbc[:, SSD_DI:SSD_DI + SSD_GN]
    c_s[...] = xbc[:, SSD_DI + SSD_GN:]
    ax_s[...] = dt_x * (-jnp.exp(alx_ref[...]))
    as_s[...] = dt_s * (-jnp.exp(als_ref[...]))

    row = _iota2((L, L), 0)
    col = _iota2((L, L), 1)
    causal = col <= row
    tri = tri_ref[...]
    eye = eye_ref[...]

    def chunk(n, carry):
        b_idx = n // cpt
        rows = pl.ds(pl.multiple_of(n * L, L), L)
        cum = _dot_sel(tri, ax_s[rows, :])
        cum_t = _dot_sel_nt(eye, _dot_sel(tri, as_s[rows, :]))
        last = cum[L - 1:L]
        e_cum = jnp.exp(cum)
        e_last = jnp.exp(last)
        xdt = xdt_s[rows, :]
        xsc = xdt * jnp.exp(last - cum)
        Bm = b_s[rows, :]
        Cm = c_s[rows, :]
        groups = range(SSD_GROUPS)
        ns = [slice(g * SSD_DSTATE, (g + 1) * SSD_DSTATE) for g in groups]
        gs = [slice(g * SSD_GW, (g + 1) * SSD_GW) for g in groups]
        sts = [s_ref[b_idx, g] for g in groups]
        CBs = [_dot_nt(Cm[:, ns[g]], Bm[:, ns[g]]) for g in groups]
        inter = [_dot(Cm[:, ns[g]], sts[g]) for g in groups]
        upd = [_dot_tn(Bm[:, ns[g]], xsc[:, gs[g]]) for g in groups]
        for g in groups:
            ys = []
            for hh in range(SSD_HPG):
                h = g * SSD_HPG + hh
                d = cum[:, h * SSD_HEADDIM:h * SSD_HEADDIM + L] - cum_t[h:h + 1, :]
                dec = jnp.exp(jnp.where(causal, d, -jnp.inf))
                ys.append(_dot(CBs[g] * dec, xdt[:, h * SSD_HEADDIM:(h + 1) * SSD_HEADDIM]))
            yo_s[rows, gs[g]] = jnp.concatenate(ys, -1) + inter[g] * e_cum[:, gs[g]]
            s_ref[b_idx, g] = sts[g] * e_last[:, gs[g]] + upd[g]
        return carry

    lax.fori_loop(0, bb * cpt, chunk, 0)

    z = jnp.dot(xb, wz_ref[...], preferred_element_type=F32)
    y = (yo_s[...] + dsk_ref[...] * xh_s[...]) * (z * _sigmoid(z))
    ms = _seg_sum(y * y, ones_ref[...], exp_ref[...]) * (1.0 / SSD_GW)
    yn = y * lax.rsqrt(ms + LN_EPS) * ng_ref[...]
    h_mix = _dot(yn, wo_ref[...])
    y_ref[...] = _layer_norm(ALPHA * x + h_mix, lng_ref[...], lnb_ref[...]).reshape(bb, tm, D_MODEL)


def _ssd_layer(x, s0, cv0, w, ln_g, ln_b, bb, tm, L):
    R = bb * tm
    tri, _ = _tile_tri(L, L)
    ones_m, exp_m = _group_mats(SSD_DI, SSD_GW)
    head_exp = (np.arange(LANES)[:, None] == np.arange(SSD_DI)[None, :] // SSD_HEADDIM).astype(np.float32)
    consts = [w["wz"], w["wxbc"], w["wdts"], w["cw"], w["cb"], w["dbs"], w["alx"],
              w["als"], w["dsk"], w["ng"], w["wo"], ln_g, ln_b,
              _const(tri), _const(np.eye(LANES, dtype=np.float32)), _const(head_exp),
              _const(ones_m), _const(exp_m)]
    scratch = [pltpu.VMEM((bb, tm + SUBLANES, SSD_CONV_DIM), F32),
               pltpu.VMEM((R, SSD_DI), F32), pltpu.VMEM((R, SSD_DI), F32),
               pltpu.VMEM((R, SSD_GN), F32), pltpu.VMEM((R, SSD_GN), F32),
               pltpu.VMEM((R, SSD_DI), F32), pltpu.VMEM((R, LANES), F32), pltpu.VMEM((R, SSD_DI), F32)]
    kern = functools.partial(_ssd_kernel, bb=bb, tm=tm, L=L)
    return _mixer_call(kern, "ssd_mixer", x, [s0, cv0], consts, [s0.shape, cv0.shape], scratch, bb, tm)


def _row(v):
    return v.reshape(1, -1).astype(F32)


def _pad_cols(w, n):
    return jnp.pad(w, ((0, 0), (0, n - w.shape[1])))


def _pad_rows(w, n):
    return jnp.pad(w, ((0, n - w.shape[0]), (0, 0)))


def _prep_gla(p, j):
    w_in = p["gla_w_in"][j]
    o_gl = 2 * GLA_KD + GLA_VD
    return {
        "wqkvr": jnp.concatenate([w_in[:, :o_gl], w_in[:, o_gl + GLA_GATE_RANK:]], 1).astype(BF16),
        "wgl": _pad_cols(w_in[:, o_gl:o_gl + GLA_GATE_RANK], LANES).astype(BF16),
        "wgu": _pad_rows(p["gla_w_gate_up"][j], LANES).astype(BF16),
        "bg": _row(p["gla_b_gate"][j]), "gng": _row(p["gla_gn_g"][j]), "gnb": _row(p["gla_gn_b"][j]),
        "wo": p["gla_w_out"][j].astype(BF16),
    }


def _prep_rwkv(p, j):
    return {
        "mu": p["rwkv_mu"][j].astype(F32),
        "wr": p["rwkv_w_rkv"][j, 0].astype(BF16), "wk": p["rwkv_w_rkv"][j, 1].astype(BF16),
        "wv": p["rwkv_w_rkv"][j, 2].astype(BF16), "wo": p["rwkv_w_o"][j].astype(BF16),
        "w0": _row(p["rwkv_w0"][j]),
        "w1": _pad_cols(p["rwkv_w1"][j], RWKV_LORA_PAD).astype(BF16),
        "w2": _pad_rows(p["rwkv_w2"][j], RWKV_LORA_PAD).astype(BF16),
        "a0": _row(p["rwkv_a0"][j]),
        "a1": _pad_cols(p["rwkv_a1"][j], RWKV_LORA_PAD).astype(BF16),
        "a2": _pad_rows(p["rwkv_a2"][j], RWKV_LORA_PAD).astype(BF16),
        "g1": p["rwkv_g1"][j].astype(BF16), "g2": p["rwkv_g2"][j].astype(BF16),
        "kk": _row(p["rwkv_k_k"][j]), "ka": _row(p["rwkv_k_a"][j]), "rk": _row(p["rwkv_r_k"][j]),
        "lxg": _row(p["rwkv_lnx_g"][j]), "lxb": _row(p["rwkv_lnx_b"][j]),
    }


def _prep_ssd(p, j):
    w_in = p["ssd_w_in"][j]
    w_dt = w_in[:, SSD_DI + SSD_CONV_DIM:]
    rep = lambda v: jnp.repeat(v, SSD_HEADDIM, axis=-1)
    return {
        "wz": w_in[:, :SSD_DI].astype(BF16),
        "wxbc": w_in[:, SSD_DI:SSD_DI + SSD_CONV_DIM].astype(BF16),
        "wdts": _pad_cols(w_dt, LANES).astype(BF16),
        "cw": p["ssd_conv_w"][j].astype(F32), "cb": _row(p["ssd_conv_b"][j]),
        "dbs": _pad_cols(_row(p["ssd_dt_bias"][j]), LANES),
        "alx": _row(rep(p["ssd_A_log"][j])), "als": _pad_cols(_row(p["ssd_A_log"][j]), LANES),
        "dsk": _row(rep(p["ssd_D"][j])), "ng": _row(p["ssd_norm_g"][j]),
        "wo": p["ssd_w_out"][j].astype(BF16),
    }


def _trunk(x, st, W, bb, tm, L, ffn_tm):
    B, T, _ = x.shape
    new = {"gla": [], "rwkv": [], "shift": [], "ssm": [], "conv": []}
    for i in range(DEPTH):
        j, kind = i // 3, i % 3
        lg, lb = W["ln_g"][i], W["ln_b"][i]
        if kind == 0:
            x, s = _gla_layer(x, st["gla"][j], W["gla"][j], lg[0], lb[0], bb, tm, L)
            new["gla"].append(s)
        elif kind == 1:
            x, s, sh = _rwkv_layer(x, st["rwkv"][j], st["shift"][j], W["rwkv"][j], lg[0], lb[0], bb, tm, L)
            new["rwkv"].append(s)
            new["shift"].append(sh)
        else:
            x, s, c = _ssd_layer(x, st["ssm"][j], st["conv"][j], W["ssd"][j], lg[0], lb[0], bb, tm, L)
            new["ssm"].append(s)
            new["conv"].append(c)
        x = _ffn_layer(x.reshape(B * T, D_MODEL), W["ffn_up"][i], W["ffn_down"][i], lg[1], lb[1],
                       ffn_tm).reshape(B, T, D_MODEL)
    return x, new


def _ssm_to_kernel(s):
    B = s.shape[0]
    return jnp.swapaxes(s.reshape(B, SSD_GROUPS, SSD_GW, SSD_DSTATE), -1, -2)


def _ssm_from_kernel(s):
    B = s.shape[0]
    return jnp.swapaxes(s, -1, -2).reshape(B, SSD_HEADS, SSD_HEADDIM, SSD_DSTATE)


def kernel(x_prompt, x_sample, state_gla, state_rwkv, state_shift, state_ssm, state_conv, meta_tokens, gla_w_in, gla_w_gate_up, gla_b_gate, gla_gn_g, gla_gn_b, gla_w_out, rwkv_mu, rwkv_w_rkv, rwkv_w_o, rwkv_w0, rwkv_w1, rwkv_w2, rwkv_a0, rwkv_a1, rwkv_a2, rwkv_g1, rwkv_g2, rwkv_k_k, rwkv_k_a, rwkv_r_k, rwkv_lnx_g, rwkv_lnx_b, ssd_w_in, ssd_conv_w, ssd_conv_b, ssd_dt_bias, ssd_A_log, ssd_D, ssd_norm_g, ssd_w_out, ffn_w_up, ffn_w_down, ln_g, ln_b):
    p = dict(gla_w_in=gla_w_in, gla_w_gate_up=gla_w_gate_up, gla_b_gate=gla_b_gate, gla_gn_g=gla_gn_g,
             gla_gn_b=gla_gn_b, gla_w_out=gla_w_out, rwkv_mu=rwkv_mu, rwkv_w_rkv=rwkv_w_rkv,
             rwkv_w_o=rwkv_w_o, rwkv_w0=rwkv_w0, rwkv_w1=rwkv_w1, rwkv_w2=rwkv_w2, rwkv_a0=rwkv_a0,
             rwkv_a1=rwkv_a1, rwkv_a2=rwkv_a2, rwkv_g1=rwkv_g1, rwkv_g2=rwkv_g2, rwkv_k_k=rwkv_k_k,
             rwkv_k_a=rwkv_k_a, rwkv_r_k=rwkv_r_k, rwkv_lnx_g=rwkv_lnx_g, rwkv_lnx_b=rwkv_lnx_b,
             ssd_w_in=ssd_w_in, ssd_conv_w=ssd_conv_w, ssd_conv_b=ssd_conv_b, ssd_dt_bias=ssd_dt_bias,
             ssd_A_log=ssd_A_log, ssd_D=ssd_D, ssd_norm_g=ssd_norm_g, ssd_w_out=ssd_w_out)
    n_gla, n_rwkv, n_ssd = gla_w_in.shape[0], rwkv_mu.shape[0], ssd_w_in.shape[0]
    W = {
        "gla": [_prep_gla(p, j) for j in range(n_gla)],
        "rwkv": [_prep_rwkv(p, j) for j in range(n_rwkv)],
        "ssd": [_prep_ssd(p, j) for j in range(n_ssd)],
        "ffn_up": [ffn_w_up[i].astype(BF16) for i in range(DEPTH)],
        "ffn_down": [ffn_w_down[i].astype(BF16) for i in range(DEPTH)],
        "ln_g": [[_row(ln_g[i, c]) for c in range(2)] for i in range(DEPTH)],
        "ln_b": [[_row(ln_b[i, c]) for c in range(2)] for i in range(DEPTH)],
    }
    Bp, Tp, _ = x_prompt.shape
    Bs, Ts, _ = x_sample.shape
    assert Ts == N_META and Tp % CHUNK == 0

    def with_meta(s):
        return jnp.concatenate([jnp.zeros((Bp,) + s.shape[1:], F32), s.astype(F32)], 0)

    x_small = jnp.concatenate(
        [jnp.broadcast_to(meta_tokens.astype(F32)[None], (Bp, N_META, D_MODEL)), x_sample.astype(F32)], 0)
    st_small = {
        "gla": [with_meta(jnp.swapaxes(state_gla[j], -1, -2)) for j in range(n_gla)],
        "rwkv": [with_meta(state_rwkv[j]) for j in range(n_rwkv)],
        "shift": [with_meta(state_shift[j]) for j in range(n_rwkv)],
        "ssm": [with_meta(_ssm_to_kernel(state_ssm[j])) for j in range(n_ssd)],
        "conv": [with_meta(state_conv[j]) for j in range(n_ssd)],
    }
    Bsm = Bp + Bs
    bb_small = 3 if Bsm % 3 == 0 else 1
    y_small, new_small = _trunk(x_small, st_small, W, bb_small, N_META, N_META, Bsm * N_META)
    st_prompt = {k: [s[:Bp] for s in v] for k, v in new_small.items()}
    tm = 256 if Tp % 256 == 0 else CHUNK
    ffn_tm = 512 if (Bp * Tp) % 512 == 0 else CHUNK
    y_prompt, new_prompt = _trunk(x_prompt.astype(F32), st_prompt, W, 1, tm, CHUNK, ffn_tm)

    def outs(new, sl):
        return (jnp.stack([jnp.swapaxes(s[sl], -1, -2) for s in new["gla"]]),
                jnp.stack([s[sl] for s in new["rwkv"]]),
                jnp.stack([s[sl] for s in new["shift"]]),
                jnp.stack([_ssm_from_kernel(s[sl]) for s in new["ssm"]]),
                jnp.stack([s[sl] for s in new["conv"]]))

    return (y_prompt, y_small[Bp:]) + outs(new_prompt, slice(None)) + outs(new_small, slice(Bp, None))
```

```python
import functools
import math

import numpy as np
import jax
import jax.numpy as jnp
from jax import lax
from jax.experimental import pallas as pl
from jax.experimental.pallas import tpu as pltpu

F32 = jnp.float32
BF16 = jnp.bfloat16

D_MODEL = 1024
DEPTH = 4
CHUNK = 64
N_META = 16
ALPHA = (2.0 * DEPTH) ** 0.25
LN_EPS = 1e-5

GLA_HEADS = 4
GLA_KD = D_MODEL // 2
GLA_VD = D_MODEL
GLA_DK = GLA_KD // GLA_HEADS
GLA_DV = GLA_VD // GLA_HEADS
GLA_GATE_RANK = 16
GLA_TAU = 16.0

RWKV_HEAD = 64
RWKV_HEADS = D_MODEL // RWKV_HEAD
RWKV_LN_EPS = 64e-5
RWKV_LORA_PAD = 128

SSD_DI = 2 * D_MODEL
SSD_HEADDIM = 64
SSD_HEADS = SSD_DI // SSD_HEADDIM
SSD_GROUPS = 8
SSD_HPG = SSD_HEADS // SSD_GROUPS
SSD_DSTATE = 128
SSD_CONV = 4
SSD_GN = SSD_GROUPS * SSD_DSTATE
SSD_CONV_DIM = SSD_DI + 2 * SSD_GN
SSD_GW = SSD_HPG * SSD_HEADDIM

FFN_HIDDEN = ((8 * D_MODEL + 3 * 256 - 1) // (3 * 256)) * 256
FFN_TH = FFN_HIDDEN // 2

LANES = 128
SUBLANES = 8
VMEM_LIMIT = 56 * 1024 * 1024


def _dot(a, b):
    return jnp.dot(a.astype(BF16), b.astype(BF16), preferred_element_type=F32)


def _dot_nt(a, b):
    return lax.dot_general(a.astype(BF16), b.astype(BF16), (((1,), (1,)), ((), ())),
                           preferred_element_type=F32)


def _dot_tn(a, b):
    return lax.dot_general(a.astype(BF16), b.astype(BF16), (((0,), (0,)), ((), ())),
                           preferred_element_type=F32)


def _split2(x):
    hi = x.astype(BF16)
    lo = (x - hi.astype(F32)).astype(BF16)
    return hi, lo


def _split3(x):
    hi = x.astype(BF16)
    r = x - hi.astype(F32)
    mid = r.astype(BF16)
    lo = (r - mid.astype(F32)).astype(BF16)
    return hi, mid, lo


def _dot_sel(w, x):
    hi, mid, lo = _split3(x)
    return (jnp.dot(w, hi, preferred_element_type=F32) + jnp.dot(w, mid, preferred_element_type=F32)
            + jnp.dot(w, lo, preferred_element_type=F32))


def _dot_sel_r(x, w):
    hi, mid, lo = _split3(x)
    return (jnp.dot(hi, w, preferred_element_type=F32) + jnp.dot(mid, w, preferred_element_type=F32)
            + jnp.dot(lo, w, preferred_element_type=F32))


def _dot_sel_nt(w, x):
    dn = (((1,), (1,)), ((), ()))
    hi, mid, lo = _split3(x)
    return (lax.dot_general(w, hi, dn, preferred_element_type=F32)
            + lax.dot_general(w, mid, dn, preferred_element_type=F32)
            + lax.dot_general(w, lo, dn, preferred_element_type=F32))


def _dot3(a, b):
    ah, al = _split2(a)
    bh, bl = _split2(b)
    return (jnp.dot(ah, bh, preferred_element_type=F32) + jnp.dot(ah, bl, preferred_element_type=F32)
            + jnp.dot(al, bh, preferred_element_type=F32))


def _seg_sum(x, ones_m, exp_m):
    hi, lo = _split2(x)
    s = jnp.dot(hi, ones_m, preferred_element_type=F32) + jnp.dot(lo, ones_m, preferred_element_type=F32)
    shi, slo = _split2(s)
    return jnp.dot(shi, exp_m, preferred_element_type=F32) + jnp.dot(slo, exp_m, preferred_element_type=F32)


def _layer_norm(x, g, b, eps=LN_EPS):
    mu = jnp.mean(x, -1, keepdims=True)
    xc = x - mu
    var = jnp.mean(xc * xc, -1, keepdims=True)
    return xc * lax.rsqrt(var + eps) * g + b


def _sigmoid(x):
    return 0.5 + 0.5 * jnp.tanh(0.5 * x)


def _softplus(x):
    return jnp.maximum(x, 0.0) + jnp.log1p(jnp.exp(-jnp.abs(x)))


def _iota2(shape, dim):
    return lax.broadcasted_iota(jnp.int32, shape, dim)


def _tile_tri(rows, L):
    i = np.arange(rows)[:, None]
    j = np.arange(rows)[None, :]
    same = (i // L) == (j // L)
    return (same & (j <= i)).astype(np.float32), same.astype(np.float32)


def _gla_selectors(L):
    i = np.arange(L)[:, None]
    j = np.arange(L)[None, :]
    mats = [j <= i, j > i]
    b = L // 2
    while b >= 1:
        blk = i // b
        odd = (blk % 2) == 1
        mats.append(np.where(odd, (j >= blk * b) & (j <= i), (j > i) & (j <= (blk + 1) * b - 1)))
        b //= 2
    return np.concatenate(mats, 0).astype(np.float32)


def _group_mats(width, group):
    i = np.arange(width)[:, None]
    j = np.arange(LANES)[None, :]
    ones = ((i // group) == j).astype(np.float32)
    return ones, ones.T.copy()


def _const(a):
    return jnp.asarray(a, BF16)


def _ffn_kernel(x_ref, wup_ref, wdn_ref, g_ref, b_ref, o_ref):
    x = x_ref[...]
    xb = x.astype(BF16)
    acc = jnp.zeros(x.shape, F32)
    for j in range(FFN_HIDDEN // FFN_TH):
        gate = jnp.dot(xb, wup_ref[:, j * FFN_TH:(j + 1) * FFN_TH], preferred_element_type=F32)
        up = jnp.dot(xb, wup_ref[:, FFN_HIDDEN + j * FFN_TH:FFN_HIDDEN + (j + 1) * FFN_TH],
                     preferred_element_type=F32)
        act = (gate * _sigmoid(gate) * up).astype(BF16)
        acc = acc + jnp.dot(act, wdn_ref[j * FFN_TH:(j + 1) * FFN_TH, :], preferred_element_type=F32)
    o_ref[...] = _layer_norm(ALPHA * x + acc, g_ref[...], b_ref[...])


def _resident(shape):
    nd = len(shape)
    return pl.BlockSpec(shape, lambda *_: (0,) * nd, pipeline_mode=pl.Buffered(1))


def _ffn_layer(x, w_up, w_down, ln_g, ln_b, tm):
    M = x.shape[0]
    return pl.pallas_call(
        _ffn_kernel,
        grid=(M // tm,),
        in_specs=[pl.BlockSpec((tm, D_MODEL), lambda i: (i, 0)),
                  _resident(w_up.shape), _resident(w_down.shape),
                  _resident(ln_g.shape), _resident(ln_b.shape)],
        out_specs=pl.BlockSpec((tm, D_MODEL), lambda i: (i, 0)),
        out_shape=jax.ShapeDtypeStruct((M, D_MODEL), F32),
        compiler_params=pltpu.CompilerParams(dimension_semantics=("arbitrary",),
                                             vmem_limit_bytes=VMEM_LIMIT),
        name="ffn_deepnorm",
    )(x, w_up, w_down, ln_g, ln_b)


def _gla_kernel(x_ref, s0_ref, wqkvr_ref, wgl_ref, wgu_ref, bg_ref, gng_ref, gnb_ref, wo_ref,
                lng_ref, lnb_ref, sel_ref, ones_ref, exp_ref,
                y_ref, s_ref, q_s, k_s, v_s, g_s, o_s, *, bb, tm, L):
    R = bb * tm
    cpt = tm // L
    nlev = int(math.log2(L))

    @pl.when(pl.program_id(1) == 0)
    def _():
        s_ref[...] = s0_ref[...]

    x = x_ref[...].reshape(R, D_MODEL)
    xb = x.astype(BF16)
    proj = jnp.dot(xb, wqkvr_ref[...], preferred_element_type=F32)
    q_s[...] = proj[:, :GLA_KD] * (GLA_DK ** -0.5)
    k_s[...] = proj[:, GLA_KD:2 * GLA_KD]
    v_s[...] = proj[:, 2 * GLA_KD:2 * GLA_KD + GLA_VD]
    r = proj[:, 2 * GLA_KD + GLA_VD:]
    gl = jnp.dot(xb, wgl_ref[...], preferred_element_type=F32)
    pre = _dot(gl, wgu_ref[...]) + bg_ref[...]
    g_s[...] = -_softplus(-pre) / GLA_TAU

    row = _iota2((L, L), 0)
    col = _iota2((L, L), 1)
    eye = row == col
    rowv = _iota2((L, 1), 0)
    pair_masks = []
    odd_rows = []
    for li in range(nlev):
        b = L >> (li + 1)
        pair_masks.append((row ^ col) < 2 * b)
        odd_rows.append((rowv & b) != 0)
    sel = sel_ref[...]

    def chunk(n, carry):
        b_idx = n // cpt
        rows = pl.ds(pl.multiple_of(n * L, L), L)
        q = q_s[rows, :]
        k = k_s[rows, :]
        v = v_s[rows, :]
        g = g_s[rows, :]
        cs = _dot_sel(sel, g)
        G = cs[0:L]
        e_last = jnp.exp(G[L - 1:L])
        qg = q * jnp.exp(G)
        kr = k * jnp.exp(cs[L:2 * L])
        qt, kt = [], []
        for li in range(nlev):
            f = jnp.exp(cs[(2 + li) * L:(3 + li) * L])
            qt.append(jnp.where(odd_rows[li], q * f, 0.0).astype(BF16))
            kt.append(jnp.where(odd_rows[li], 0.0, k * f).astype(BF16))
        heads = range(GLA_HEADS)
        ks = [slice(h * GLA_DK, (h + 1) * GLA_DK) for h in heads]
        vs = [slice(h * GLA_DV, (h + 1) * GLA_DV) for h in heads]
        sts = [s_ref[b_idx, h] for h in heads]
        scores = [[_dot_nt(qt[li][:, ks[h]], kt[li][:, ks[h]]) for li in range(nlev)] for h in heads]
        inter = [_dot_nt(qg[:, ks[h]], sts[h]) for h in heads]
        upd = [_dot_tn(v[:, vs[h]], kr[:, ks[h]]) for h in heads]
        for h in heads:
            A = jnp.where(eye, jnp.sum(q[:, ks[h]] * k[:, ks[h]], -1, keepdims=True), 0.0)
            for li in range(nlev):
                A = A + jnp.where(pair_masks[li], scores[h][li], 0.0)
            o_s[rows, vs[h]] = inter[h] + _dot(A, v[:, vs[h]])
            s_ref[b_idx, h] = sts[h] * e_last[:, ks[h]] + upd[h]
        return carry

    lax.fori_loop(0, bb * cpt, chunk, 0)

    o = o_s[...]
    mu = _seg_sum(o, ones_ref[...], exp_ref[...]) * (1.0 / GLA_DV)
    oc = o - mu
    var = _seg_sum(oc * oc, ones_ref[...], exp_ref[...]) * (1.0 / GLA_DV)
    on = oc * lax.rsqrt(var + LN_EPS) * gng_ref[...] + gnb_ref[...]
    h_mix = _dot(on * (r * _sigmoid(r)), wo_ref[...])
    y_ref[...] = _layer_norm(ALPHA * x + h_mix, lng_ref[...], lnb_ref[...]).reshape(bb, tm, D_MODEL)


def _mixer_call(kernel, name, x, states, consts, out_state_shapes, scratch, bb, tm):
    B, T, _ = x.shape
    grid = (B // bb, T // tm)

    def state_spec(shape):
        nd = len(shape)
        return pl.BlockSpec((bb,) + tuple(shape[1:]), lambda b, t: (b,) + (0,) * (nd - 1))

    in_specs = ([pl.BlockSpec((bb, tm, D_MODEL), lambda b, t: (b, t, 0))]
                + [state_spec(s.shape) for s in states]
                + [_resident(c.shape) for c in consts])
    out_specs = ([pl.BlockSpec((bb, tm, D_MODEL), lambda b, t: (b, t, 0))]
                 + [state_spec(s) for s in out_state_shapes])
    out_shape = ([jax.ShapeDtypeStruct((B, T, D_MODEL), F32)]
                 + [jax.ShapeDtypeStruct(s, F32) for s in out_state_shapes])
    return pl.pallas_call(
        kernel, grid=grid, in_specs=in_specs, out_specs=out_specs, out_shape=out_shape,
        scratch_shapes=scratch,
        compiler_params=pltpu.CompilerParams(dimension_semantics=("arbitrary", "arbitrary"),
                                             vmem_limit_bytes=VMEM_LIMIT),
        name=name,
    )(x, *states, *consts)


def _gla_layer(x, s0, w, ln_g, ln_b, bb, tm, L):
    R = bb * tm
    ones_m, exp_m = _group_mats(GLA_VD, GLA_DV)
    consts = [w["wqkvr"], w["wgl"], w["wgu"], w["bg"], w["gng"], w["gnb"], w["wo"], ln_g, ln_b,
              _const(_gla_selectors(L)), _const(ones_m), _const(exp_m)]
    scratch = [pltpu.VMEM((R, GLA_KD), F32), pltpu.VMEM((R, GLA_KD), F32), pltpu.VMEM((R, GLA_VD), F32),
               pltpu.VMEM((R, GLA_KD), F32), pltpu.VMEM((R, GLA_VD), F32)]
    kern = functools.partial(_gla_kernel, bb=bb, tm=tm, L=L)
    return _mixer_call(kern, "gla_mixer", x, [s0], consts, [s0.shape], scratch, bb, tm)


def _lo_half(shape):
    return _iota2(shape, 1) < shape[1] // 2


def _bd(z):
    lo = _lo_half(z.shape)
    zero = jnp.zeros_like(z)
    return jnp.concatenate([jnp.where(lo, z, zero), jnp.where(lo, zero, z)], 0)


def _dot3_bd_many(ys, z):
    M = ys[0].shape[0]
    n = len(ys)
    parts = [_split2(y) for y in ys]
    his = [h for h, _ in parts]
    zh, zl = _split2(z)
    hh = jnp.dot(jnp.concatenate(his + [l for _, l in parts], 0), _bd(zh), preferred_element_type=F32)
    hl = jnp.dot(jnp.concatenate(his, 0) if n > 1 else his[0], _bd(zl), preferred_element_type=F32)
    return [hh[i * M:(i + 1) * M] + hh[(n + i) * M:(n + i + 1) * M] + hl[i * M:(i + 1) * M] for i in range(n)]


def _dot3_bd(y, z):
    return _dot3_bd_many([y], z)[0]


def _tri_inverse_pairs(Ts, L):
    row = _iota2((L, 2 * L), 0)
    col = _iota2((L, 2 * L), 1) & (L - 1)
    eye = (row == col).astype(F32)
    bs = min(16, L)
    Ps = [jnp.where((row ^ col) < bs, T, 0.0) for T in Ts]
    Xs = [eye - P for P in Ps]
    if bs > 2:
        Ps = [_dot3_bd(P, P) for P in Ps]
    p = 2
    while p < bs:
        if 2 * p < bs:
            both = [_dot3_bd_many([X, P], P) for X, P in zip(Xs, Ps)]
            Xs = [X + xp for X, (xp, _) in zip(Xs, both)]
            Ps = [pp for _, pp in both]
        else:
            Xs = [X + _dot3_bd(X, P) for X, P in zip(Xs, Ps)]
        p *= 2
    s = bs
    while s < L:
        off = ((row ^ col) < 2 * s) & ((row ^ col) >= s)
        XC = [_dot3_bd(X, jnp.where(off, T, 0.0)) for X, T in zip(Xs, Ts)]
        Xs = [X - _dot3_bd(Y, X) for X, Y in zip(Xs, XC)]
        s *= 2
    return Xs


def _rwkv_kernel(x_ref, s0_ref, sh0_ref, mu_ref, wr_ref, wk_ref, wv_ref, wo_ref, w0_ref, w1_ref, w2_ref,
                 a0_ref, a1_ref, a2_ref, g1_ref, g2_ref, kk_ref, ka_ref, rk_ref, lxg_ref, lxb_ref,
                 lng_ref, lnb_ref, tri_ref, tot_ref, ones_ref, exp_ref,
                 y_ref, s_ref, sh_ref, xs_s, pa_s, xr_s, wy_s, yh_s, *, bb, tm, L):
    R = bb * tm
    cpt = tm // L
    H, N = RWKV_HEADS, RWKV_HEAD

    @pl.when(pl.program_id(1) == 0)
    def _():
        s_ref[...] = s0_ref[...]
        xs_s[:, SUBLANES - 1:SUBLANES, :] = sh0_ref[...]

    x3 = x_ref[...]
    xs_s[:, SUBLANES:SUBLANES + tm, :] = x3
    xprev = xs_s[:, SUBLANES - 1:SUBLANES - 1 + tm, :].reshape(R, D_MODEL)
    last = x3[:, tm - 1:tm, :]
    xs_s[:, SUBLANES - 1:SUBLANES, :] = last
    sh_ref[...] = last
    x = x3.reshape(R, D_MODEL)
    xx = xprev - x
    mu = mu_ref[...]

    def mix(c):
        return (x + xx * mu[c:c + 1, :]).astype(BF16)

    r = jnp.dot(mix(0), wr_ref[...], preferred_element_type=F32)
    k = jnp.dot(mix(2), wk_ref[...], preferred_element_type=F32)
    v = jnp.dot(mix(3), wv_ref[...], preferred_element_type=F32)
    wl = jnp.tanh(jnp.dot(mix(1), w1_ref[...], preferred_element_type=F32))
    w_log = -_softplus(-(w0_ref[...] + _dot(wl, w2_ref[...]))) - 0.5
    lw = -jnp.exp(w_log)
    al = jnp.dot(mix(4), a1_ref[...], preferred_element_type=F32)
    a = _sigmoid(a0_ref[...] + _dot(al, a2_ref[...]))
    gg = _dot(_sigmoid(jnp.dot(mix(5), g1_ref[...], preferred_element_type=F32)), g2_ref[...])

    ones_m = ones_ref[...]
    exp_m = exp_ref[...]
    kk = k * kk_ref[...]
    kk = kk * lax.rsqrt(_seg_sum(kk * kk, ones_m, exp_m) + 1e-12)
    k = k * (1.0 + (a - 1.0) * ka_ref[...])
    bvec = kk * a

    gam = _dot_sel(tri_ref[...], lw)
    gl = _dot_sel(tot_ref[...], lw)
    e_neg = jnp.exp(-gam)
    e_rem = jnp.exp(gl - gam)
    alpha = kk * jnp.exp(gam - lw)
    rho = r * jnp.exp(gam)
    beta = bvec * e_neg
    kap = k * e_neg
    kap_l = k * e_rem
    bet_l = -(bvec * e_rem)
    e_tot = jnp.exp(gl)
    PW = 2 * N
    NP = H // 2
    for p in range(NP):
        ps = slice(p * PW, (p + 1) * PW)
        for slot, arr in enumerate((alpha, rho, beta, kap, v, kap_l, bet_l, e_tot)):
            pa_s[p, slot] = arr[:, ps]

    rowp = _iota2((L, 2 * L), 0)
    colp = _iota2((L, 2 * L), 1) & (L - 1)
    strict = colp < rowp
    incl = colp <= rowp
    lo_state = _lo_half((N, PW))

    def chunk_rows(n):
        return pl.ds(pl.multiple_of(n * L, L), L)

    n_chunks = bb * cpt
    ca = 2 if n_chunks % 2 == 0 else 1

    def prepare(i, carry):
        items = [(chunk_rows(i * ca + c), p) for c in range(ca) for p in range(NP)]
        ARs = [jnp.concatenate([pa_s[p, 0, rows, :], pa_s[p, 1, rows, :]], 0) for rows, p in items]
        Ms = [_dot_nt(AR, jnp.concatenate([_bd(pa_s[p, 2, rows, :]), _bd(pa_s[p, 3, rows, :])], 0))
              for AR, (rows, p) in zip(ARs, items)]
        vbd = [_bd(pa_s[p, 4, rows, :].astype(BF16)) for rows, p in items]
        Xs = _tri_inverse_pairs([jnp.where(strict, M[:L, :2 * L], 0.0) for M in Ms], L)
        for (rows, p), M, X, vb in zip(items, Ms, Xs, vbd):
            xr_s[p, 0, rows, :] = X
            xr_s[p, 1, rows, :] = jnp.where(incl, M[L:, :2 * L], 0.0)
            wy_s[p, 0, rows, :] = _dot(jnp.where(strict, M[:L, 2 * L:], 0.0), vb)
            wy_s[p, 1, rows, :] = _dot(jnp.where(incl, M[L:, 2 * L:], 0.0), vb)
        return carry

    lax.fori_loop(0, n_chunks // ca, prepare, 0)

    def recur(c, carry):
        items = [(b, chunk_rows(b * cpt + c), p) for b in range(bb) for p in range(NP)]
        Ss = [s_ref[b, p] for b, _, p in items]
        W0s = [_dot_nt(jnp.concatenate([pa_s[p, 0, rows, :], pa_s[p, 1, rows, :]], 0), _bd(S))
               for (_, rows, p), S in zip(items, Ss)]
        Us = [_dot3_bd(xr_s[p, 0, rows, :], W0[:L] + wy_s[p, 0, rows, :])
              for (_, rows, p), W0 in zip(items, W0s)]
        for (b, rows, p), S, W0, U in zip(items, Ss, W0s, Us):
            yh_s[p, rows, :] = W0[L:] - _dot(xr_s[p, 1, rows, :], _bd(U.astype(BF16))) + wy_s[p, 1, rows, :]
            kb = jnp.concatenate([pa_s[p, 5, rows, :], pa_s[p, 6, rows, :]], 0)
            cross = _dot_tn(jnp.concatenate([pa_s[p, 4, rows, :], U], 0), kb)
            e_row = pa_s[p, 7, pl.ds(pl.multiple_of((b * cpt + c) * L, L), 1), :]
            s_ref[b, p] = S * e_row + jnp.where(lo_state, cross[:N], cross[N:])
        return carry

    lax.fori_loop(0, cpt, recur, 0)

    y = jnp.concatenate([yh_s[p] for p in range(NP)], -1)
    mu_y = _seg_sum(y, ones_m, exp_m) * (1.0 / N)
    yc = y - mu_y
    var_y = _seg_sum(yc * yc, ones_m, exp_m) * (1.0 / N)
    yn = yc * lax.rsqrt(var_y + RWKV_LN_EPS) * lxg_ref[...] + lxb_ref[...]
    bonus = _seg_sum(r * k * rk_ref[...], ones_m, exp_m) * v
    h_mix = _dot((yn + bonus) * gg, wo_ref[...])
    y_ref[...] = _layer_norm(ALPHA * x + h_mix, lng_ref[...], lnb_ref[...]).reshape(bb, tm, D_MODEL)


def _rwkv_layer(x, s0, sh0, w, ln_g, ln_b, bb, tm, L):
    R = bb * tm
    H, N = RWKV_HEADS, RWKV_HEAD
    tri, tot = _tile_tri(R, L)
    ones_m, exp_m = _group_mats(D_MODEL, N)
    consts = [w["mu"], w["wr"], w["wk"], w["wv"], w["wo"], w["w0"], w["w1"], w["w2"], w["a0"], w["a1"],
              w["a2"], w["g1"], w["g2"], w["kk"], w["ka"], w["rk"], w["lxg"], w["lxb"], ln_g, ln_b,
              _const(tri), _const(tot), _const(ones_m), _const(exp_m)]
    scratch = [pltpu.VMEM((bb, tm + SUBLANES, D_MODEL), F32),
               pltpu.VMEM((H // 2, 8, R, 2 * N), F32), pltpu.VMEM((H // 2, 2, R, 2 * L), F32),
               pltpu.VMEM((H // 2, 2, R, 2 * N), F32), pltpu.VMEM((H // 2, R, 2 * N), F32)]
    kern = functools.partial(_rwkv_kernel, bb=bb, tm=tm, L=L)
    return _mixer_call(kern, "rwkv_mixer", x, [s0, sh0], consts, [s0.shape, sh0.shape], scratch, bb, tm)


def _ssd_kernel(x_ref, s0_ref, cv0_ref, wz_ref, wxbc_ref, wdts_ref, cw_ref, cb_ref,
                dbs_ref, alx_ref, als_ref, dsk_ref, ng_ref, wo_ref, lng_ref, lnb_ref,
                tri_ref, eye_ref, hexp_ref, ones_ref, exp_ref,
                y_ref, s_ref, cv_ref, xs_s, xh_s, xdt_s, b_s, c_s, ax_s, as_s, yo_s, *, bb, tm, L):
    R = bb * tm
    cpt = tm // L
    W = SSD_CONV
    PAD = SUBLANES

    @pl.when(pl.program_id(1) == 0)
    def _():
        s_ref[...] = s0_ref[...]
        xs_s[:, PAD - (W - 1):PAD, :] = cv0_ref[...]

    x = x_ref[...].reshape(R, D_MODEL)
    xb = x.astype(BF16)
    xbc_raw = jnp.dot(xb, wxbc_ref[...], preferred_element_type=F32)
    xs_s[:, PAD:PAD + tm, :] = xbc_raw.reshape(bb, tm, SSD_CONV_DIM)
    cw = cw_ref[...]
    conv = cb_ref[...].reshape(1, 1, SSD_CONV_DIM)
    for wi in range(W):
        lo = PAD - (W - 1) + wi
        conv = conv + xs_s[:, lo:lo + tm, :] * cw[wi:wi + 1, :].reshape(1, 1, SSD_CONV_DIM)
    tail = xs_s[:, PAD + tm - (W - 1):PAD + tm, :]
    xs_s[:, PAD - (W - 1):PAD, :] = tail
    cv_ref[...] = tail
    conv = conv.reshape(R, SSD_CONV_DIM)
    xbc = conv * _sigmoid(conv)
    xh = xbc[:, :SSD_DI]
    dt_s = _softplus(jnp.dot(xb, wdts_ref[...], preferred_element_type=F32) + dbs_ref[...])
    dt_x = _dot_sel_r(dt_s, hexp_ref[...])
    xh_s[...] = xh
    xdt_s[...] = xh * dt_x
    b_s[...] = xbc[:, SSD_DI:SSD_DI + SSD_GN]
    c_s[...] = xbc[:, SSD_DI + SSD_GN:]
    ax_s[...] = dt_x * (-jnp.exp(alx_ref[...]))
    as_s[...] = dt_s * (-jnp.exp(als_ref[...]))

    row = _iota2((L, L), 0)
    col = _iota2((L, L), 1)
    causal = col <= row
    tri = tri_ref[...]
    eye = eye_ref[...]

    def chunk(n, carry):
        b_idx = n // cpt
        rows = pl.ds(pl.multiple_of(n * L, L), L)
        cum = _dot_sel(tri, ax_s[rows, :])
        cum_t = _dot_sel_nt(eye, _dot_sel(tri, as_s[rows, :]))
        last = cum[L - 1:L]
        e_cum = jnp.exp(cum)
        e_last = jnp.exp(last)
        xdt = xdt_s[rows, :]
        xsc = xdt * jnp.exp(last - cum)
        Bm = b_s[rows, :]
        Cm = c_s[rows, :]
        groups = range(SSD_GROUPS)
        ns = [slice(g * SSD_DSTATE, (g + 1) * SSD_DSTATE) for g in groups]
        gs = [slice(g * SSD_GW, (g + 1) * SSD_GW) for g in groups]
        sts = [s_ref[b_idx, g] for g in groups]
        CBs = [_dot_nt(Cm[:, ns[g]], Bm[:, ns[g]]) for g in groups]
        inter = [_dot(Cm[:, ns[g]], sts[g]) for g in groups]
        upd = [_dot_tn(Bm[:, ns[g]], xsc[:, gs[g]]) for g in groups]
        for g in groups:
            ys = []
            for hh in range(SSD_HPG):
                h = g * SSD_HPG + hh
                d = cum[:, h * SSD_HEADDIM:h * SSD_HEADDIM + L] - cum_t[h:h + 1, :]
                dec = jnp.exp(jnp.where(causal, d, -jnp.inf))
                ys.append(_dot(CBs[g] * dec, xdt[:, h * SSD_HEADDIM:(h + 1) * SSD_HEADDIM]))
            yo_s[rows, gs[g]] = jnp.concatenate(ys, -1) + inter[g] * e_cum[:, gs[g]]
            s_ref[b_idx, g] = sts[g] * e_last[:, gs[g]] + upd[g]
        return carry

    lax.fori_loop(0, bb * cpt, chunk, 0)

    z = jnp.dot(xb, wz_ref[...], preferred_element_type=F32)
    y = (yo_s[...] + dsk_ref[...] * xh_s[...]) * (z * _sigmoid(z))
    ms = _seg_sum(y * y, ones_ref[...], exp_ref[...]) * (1.0 / SSD_GW)
    yn = y * lax.rsqrt(ms + LN_EPS) * ng_ref[...]
    h_mix = _dot(yn, wo_ref[...])
    y_ref[...] = _layer_norm(ALPHA * x + h_mix, lng_ref[...], lnb_ref[...]).reshape(bb, tm, D_MODEL)


def _ssd_layer(x, s0, cv0, w, ln_g, ln_b, bb, tm, L):
    R = bb * tm
    tri, _ = _tile_tri(L, L)
    ones_m, exp_m = _group_mats(SSD_DI, SSD_GW)
    head_exp = (np.arange(LANES)[:, None] == np.arange(SSD_DI)[None, :] // SSD_HEADDIM).astype(np.float32)
    consts = [w["wz"], w["wxbc"], w["wdts"], w["cw"], w["cb"], w["dbs"], w["alx"],
              w["als"], w["dsk"], w["ng"], w["wo"], ln_g, ln_b,
              _const(tri), _const(np.eye(LANES, dtype=np.float32)), _const(head_exp),
              _const(ones_m), _const(exp_m)]
    scratch = [pltpu.VMEM((bb, tm + SUBLANES, SSD_CONV_DIM), F32),
               pltpu.VMEM((R, SSD_DI), F32), pltpu.VMEM((R, SSD_DI), F32),
               pltpu.VMEM((R, SSD_GN), F32), pltpu.VMEM((R, SSD_GN), F32),
               pltpu.VMEM((R, SSD_DI), F32), pltpu.VMEM((R, LANES), F32), pltpu.VMEM((R, SSD_DI), F32)]
    kern = functools.partial(_ssd_kernel, bb=bb, tm=tm, L=L)
    return _mixer_call(kern, "ssd_mixer", x, [s0, cv0], consts, [s0.shape, cv0.shape], scratch, bb, tm)


def _row(v):
    return v.reshape(1, -1).astype(F32)


def _pad_cols(w, n):
    return jnp.pad(w, ((0, 0), (0, n - w.shape[1])))


def _pad_rows(w, n):
    return jnp.pad(w, ((0, n - w.shape[0]), (0, 0)))


def _prep_gla(p, j):
    w_in = p["gla_w_in"][j]
    o_gl = 2 * GLA_KD + GLA_VD
    return {
        "wqkvr": jnp.concatenate([w_in[:, :o_gl], w_in[:, o_gl + GLA_GATE_RANK:]], 1).astype(BF16),
        "wgl": _pad_cols(w_in[:, o_gl:o_gl + GLA_GATE_RANK], LANES).astype(BF16),
        "wgu": _pad_rows(p["gla_w_gate_up"][j], LANES).astype(BF16),
        "bg": _row(p["gla_b_gate"][j]), "gng": _row(p["gla_gn_g"][j]), "gnb": _row(p["gla_gn_b"][j]),
        "wo": p["gla_w_out"][j].astype(BF16),
    }


def _prep_rwkv(p, j):
    return {
        "mu": p["rwkv_mu"][j].astype(F32),
        "wr": p["rwkv_w_rkv"][j, 0].astype(BF16), "wk": p["rwkv_w_rkv"][j, 1].astype(BF16),
        "wv": p["rwkv_w_rkv"][j, 2].astype(BF16), "wo": p["rwkv_w_o"][j].astype(BF16),
        "w0": _row(p["rwkv_w0"][j]),
        "w1": _pad_cols(p["rwkv_w1"][j], RWKV_LORA_PAD).astype(BF16),
        "w2": _pad_rows(p["rwkv_w2"][j], RWKV_LORA_PAD).astype(BF16),
        "a0": _row(p["rwkv_a0"][j]),
        "a1": _pad_cols(p["rwkv_a1"][j], RWKV_LORA_PAD).astype(BF16),
        "a2": _pad_rows(p["rwkv_a2"][j], RWKV_LORA_PAD).astype(BF16),
        "g1": p["rwkv_g1"][j].astype(BF16), "g2": p["rwkv_g2"][j].astype(BF16),
        "kk": _row(p["rwkv_k_k"][j]), "ka": _row(p["rwkv_k_a"][j]), "rk": _row(p["rwkv_r_k"][j]),
        "lxg": _row(p["rwkv_lnx_g"][j]), "lxb": _row(p["rwkv_lnx_b"][j]),
    }


def _prep_ssd(p, j):
    w_in = p["ssd_w_in"][j]
    w_dt = w_in[:, SSD_DI + SSD_CONV_DIM:]
    rep = lambda v: jnp.repeat(v, SSD_HEADDIM, axis=-1)
    return {
        "wz": w_in[:, :SSD_DI].astype(BF16),
        "wxbc": w_in[:, SSD_DI:SSD_DI + SSD_CONV_DIM].astype(BF16),
        "wdts": _pad_cols(w_dt, LANES).astype(BF16),
        "cw": p["ssd_conv_w"][j].astype(F32), "cb": _row(p["ssd_conv_b"][j]),
        "dbs": _pad_cols(_row(p["ssd_dt_bias"][j]), LANES),
        "alx": _row(rep(p["ssd_A_log"][j])), "als": _pad_cols(_row(p["ssd_A_log"][j]), LANES),
        "dsk": _row(rep(p["ssd_D"][j])), "ng": _row(p["ssd_norm_g"][j]),
        "wo": p["ssd_w_out"][j].astype(BF16),
    }


def _trunk(x, st, W, tiles, L, ffn_tm):
    B, T, _ = x.shape
    new = {"gla": [], "rwkv": [], "shift": [], "ssm": [], "conv": []}
    for i in range(DEPTH):
        j, kind = i // 3, i % 3
        lg, lb = W["ln_g"][i], W["ln_b"][i]
        bb, tm = tiles[kind]
        if kind == 0:
            x, s = _gla_layer(x, st["gla"][j], W["gla"][j], lg[0], lb[0], bb, tm, L)
            new["gla"].append(s)
        elif kind == 1:
            x, s, sh = _rwkv_layer(x, st["rwkv"][j], st["shift"][j], W["rwkv"][j], lg[0], lb[0], bb, tm, L)
            new["rwkv"].append(s)
            new["shift"].append(sh)
        else:
            x, s, c = _ssd_layer(x, st["ssm"][j], st["conv"][j], W["ssd"][j], lg[0], lb[0], bb, tm, L)
            new["ssm"].append(s)
            new["conv"].append(c)
        x = _ffn_layer(x.reshape(B * T, D_MODEL), W["ffn_up"][i], W["ffn_down"][i], lg[1], lb[1],
                       ffn_tm).reshape(B, T, D_MODEL)
    return x, new


def _rwkv_to_kernel(s):
    B = s.shape[0]
    s = s.reshape(B, RWKV_HEADS // 2, 2, RWKV_HEAD, RWKV_HEAD)
    return jnp.swapaxes(s, 2, 3).reshape(B, RWKV_HEADS // 2, RWKV_HEAD, 2 * RWKV_HEAD)


def _rwkv_from_kernel(s):
    B = s.shape[0]
    s = s.reshape(B, RWKV_HEADS // 2, RWKV_HEAD, 2, RWKV_HEAD)
    return jnp.swapaxes(s, 2, 3).reshape(B, RWKV_HEADS, RWKV_HEAD, RWKV_HEAD)


def _ssm_to_kernel(s):
    B = s.shape[0]
    return jnp.swapaxes(s.reshape(B, SSD_GROUPS, SSD_GW, SSD_DSTATE), -1, -2)


def _ssm_from_kernel(s):
    B = s.shape[0]
    return jnp.swapaxes(s, -1, -2).reshape(B, SSD_HEADS, SSD_HEADDIM, SSD_DSTATE)


def kernel(x_prompt, x_sample, state_gla, state_rwkv, state_shift, state_ssm, state_conv, meta_tokens, gla_w_in, gla_w_gate_up, gla_b_gate, gla_gn_g, gla_gn_b, gla_w_out, rwkv_mu, rwkv_w_rkv, rwkv_w_o, rwkv_w0, rwkv_w1, rwkv_w2, rwkv_a0, rwkv_a1, rwkv_a2, rwkv_g1, rwkv_g2, rwkv_k_k, rwkv_k_a, rwkv_r_k, rwkv_lnx_g, rwkv_lnx_b, ssd_w_in, ssd_conv_w, ssd_conv_b, ssd_dt_bias, ssd_A_log, ssd_D, ssd_norm_g, ssd_w_out, ffn_w_up, ffn_w_down, ln_g, ln_b):
    p = dict(gla_w_in=gla_w_in, gla_w_gate_up=gla_w_gate_up, gla_b_gate=gla_b_gate, gla_gn_g=gla_gn_g,
             gla_gn_b=gla_gn_b, gla_w_out=gla_w_out, rwkv_mu=rwkv_mu, rwkv_w_rkv=rwkv_w_rkv,
             rwkv_w_o=rwkv_w_o, rwkv_w0=rwkv_w0, rwkv_w1=rwkv_w1, rwkv_w2=rwkv_w2, rwkv_a0=rwkv_a0,
             rwkv_a1=rwkv_a1, rwkv_a2=rwkv_a2, rwkv_g1=rwkv_g1, rwkv_g2=rwkv_g2, rwkv_k_k=rwkv_k_k,
             rwkv_k_a=rwkv_k_a, rwkv_r_k=rwkv_r_k, rwkv_lnx_g=rwkv_lnx_g, rwkv_lnx_b=rwkv_lnx_b,
             ssd_w_in=ssd_w_in, ssd_conv_w=ssd_conv_w, ssd_conv_b=ssd_conv_b, ssd_dt_bias=ssd_dt_bias,
             ssd_A_log=ssd_A_log, ssd_D=ssd_D, ssd_norm_g=ssd_norm_g, ssd_w_out=ssd_w_out)
    n_gla, n_rwkv, n_ssd = gla_w_in.shape[0], rwkv_mu.shape[0], ssd_w_in.shape[0]
    W = {
        "gla": [_prep_gla(p, j) for j in range(n_gla)],
        "rwkv": [_prep_rwkv(p, j) for j in range(n_rwkv)],
        "ssd": [_prep_ssd(p, j) for j in range(n_ssd)],
        "ffn_up": [ffn_w_up[i].astype(BF16) for i in range(DEPTH)],
        "ffn_down": [ffn_w_down[i].astype(BF16) for i in range(DEPTH)],
        "ln_g": [[_row(ln_g[i, c]) for c in range(2)] for i in range(DEPTH)],
        "ln_b": [[_row(ln_b[i, c]) for c in range(2)] for i in range(DEPTH)],
    }
    Bp, Tp, _ = x_prompt.shape
    Bs, Ts, _ = x_sample.shape
    assert Ts == N_META and Tp % CHUNK == 0

    def with_meta(s):
        return jnp.concatenate([jnp.zeros((Bp,) + s.shape[1:], F32), s.astype(F32)], 0)

    x_small = jnp.concatenate(
        [jnp.broadcast_to(meta_tokens.astype(F32)[None], (Bp, N_META, D_MODEL)), x_sample.astype(F32)], 0)
    st_small = {
        "gla": [with_meta(jnp.swapaxes(state_gla[j], -1, -2)) for j in range(n_gla)],
        "rwkv": [with_meta(_rwkv_to_kernel(state_rwkv[j])) for j in range(n_rwkv)],
        "shift": [with_meta(state_shift[j]) for j in range(n_rwkv)],
        "ssm": [with_meta(_ssm_to_kernel(state_ssm[j])) for j in range(n_ssd)],
        "conv": [with_meta(state_conv[j]) for j in range(n_ssd)],
    }
    Bsm = Bp + Bs
    bb_small = 3 if Bsm % 3 == 0 else 1
    y_small, new_small = _trunk(x_small, st_small, W, [(bb_small, N_META)] * 3, N_META, Bsm * N_META)
    st_prompt = {k: [s[:Bp] for s in v] for k, v in new_small.items()}
    tm = 256 if Tp % 256 == 0 else CHUNK
    ffn_tm = 512 if (Bp * Tp) % 512 == 0 else CHUNK
    tiles = [(1, tm), (Bp, max(tm // Bp, CHUNK)), (1, tm)]
    y_prompt, new_prompt = _trunk(x_prompt.astype(F32), st_prompt, W, tiles, CHUNK, ffn_tm)

    def outs(new, sl):
        return (jnp.stack([jnp.swapaxes(s[sl], -1, -2) for s in new["gla"]]),
                jnp.stack([_rwkv_from_kernel(s[sl]) for s in new["rwkv"]]),
                jnp.stack([s[sl] for s in new["shift"]]),
                jnp.stack([_ssm_from_kernel(s[sl]) for s in new["ssm"]]),
                jnp.stack([s[sl] for s in new["conv"]]))

    return (y_prompt, y_small[Bp:]) + outs(new_prompt, slice(None)) + outs(new_small, slice(Bp, None))
```

```python
import functools
import math

import numpy as np
import jax
import jax.numpy as jnp
from jax import lax
from jax.experimental import pallas as pl
from jax.experimental.pallas import tpu as pltpu

F32 = jnp.float32
BF16 = jnp.bfloat16

D_MODEL = 1024
DEPTH = 4
CHUNK = 64
N_META = 16
ALPHA = (2.0 * DEPTH) ** 0.25
LN_EPS = 1e-5

GLA_HEADS = 4
GLA_KD = D_MODEL // 2
GLA_VD = D_MODEL
GLA_DK = GLA_KD // GLA_HEADS
GLA_DV = GLA_VD // GLA_HEADS
GLA_GATE_RANK = 16
GLA_TAU = 16.0

RWKV_HEAD = 64
RWKV_HEADS = D_MODEL // RWKV_HEAD
RWKV_LN_EPS = 64e-5
RWKV_LORA_PAD = 128

SSD_DI = 2 * D_MODEL
SSD_HEADDIM = 64
SSD_HEADS = SSD_DI // SSD_HEADDIM
SSD_GROUPS = 8
SSD_HPG = SSD_HEADS // SSD_GROUPS
SSD_DSTATE = 128
SSD_CONV = 4
SSD_GN = SSD_GROUPS * SSD_DSTATE
SSD_CONV_DIM = SSD_DI + 2 * SSD_GN
SSD_GW = SSD_HPG * SSD_HEADDIM

FFN_HIDDEN = ((8 * D_MODEL + 3 * 256 - 1) // (3 * 256)) * 256
FFN_TH = 256

LANES = 128
SUBLANES = 8
VMEM_LIMIT = 56 * 1024 * 1024


def _dot(a, b):
    return jnp.dot(a.astype(BF16), b.astype(BF16), preferred_element_type=F32)


def _dot_nt(a, b):
    return lax.dot_general(a.astype(BF16), b.astype(BF16), (((1,), (1,)), ((), ())),
                           preferred_element_type=F32)


def _dot_tn(a, b):
    return lax.dot_general(a.astype(BF16), b.astype(BF16), (((0,), (0,)), ((), ())),
                           preferred_element_type=F32)


def _split2(x):
    hi = x.astype(BF16)
    lo = (x - hi.astype(F32)).astype(BF16)
    return hi, lo


def _split3(x):
    hi = x.astype(BF16)
    r = x - hi.astype(F32)
    mid = r.astype(BF16)
    lo = (r - mid.astype(F32)).astype(BF16)
    return hi, mid, lo


def _dot_sel_k3(w3, x):
    return jnp.dot(w3, jnp.concatenate(_split3(x), 0), preferred_element_type=F32)


def _dot_sel(w, x):
    hi, mid, lo = _split3(x)
    return (jnp.dot(w, hi, preferred_element_type=F32) + jnp.dot(w, mid, preferred_element_type=F32)
            + jnp.dot(w, lo, preferred_element_type=F32))


def _dot_sel_r(x, w2):
    hi, mid, lo = _split3(x)
    return (jnp.dot(jnp.concatenate([hi, mid], -1), w2, preferred_element_type=F32)
            + jnp.dot(lo, w2[:x.shape[1]], preferred_element_type=F32))


def _dot_sel_nt(w, x):
    dn = (((1,), (1,)), ((), ()))
    hi, mid, lo = _split3(x)
    return (lax.dot_general(w, hi, dn, preferred_element_type=F32)
            + lax.dot_general(w, mid, dn, preferred_element_type=F32)
            + lax.dot_general(w, lo, dn, preferred_element_type=F32))


def _dot3(a, b):
    ah, al = _split2(a)
    bh, bl = _split2(b)
    return (jnp.dot(ah, bh, preferred_element_type=F32) + jnp.dot(ah, bl, preferred_element_type=F32)
            + jnp.dot(al, bh, preferred_element_type=F32))


def _seg_sum(x, ones_m, exp_m):
    hi, lo = _split2(x)
    s = jnp.dot(hi, ones_m, preferred_element_type=F32) + jnp.dot(lo, ones_m, preferred_element_type=F32)
    return jnp.dot(jnp.concatenate(_split2(s), -1), exp_m, preferred_element_type=F32)


def _layer_norm(x, g, b, eps=LN_EPS):
    mu = jnp.mean(x, -1, keepdims=True)
    xc = x - mu
    var = jnp.mean(xc * xc, -1, keepdims=True)
    return xc * lax.rsqrt(var + eps) * g + b


def _sigmoid(x):
    return 0.5 + 0.5 * jnp.tanh(0.5 * x)


def _softplus(x):
    return jnp.maximum(x, 0.0) + jnp.log1p(jnp.exp(-jnp.abs(x)))


def _iota2(shape, dim):
    return lax.broadcasted_iota(jnp.int32, shape, dim)


def _tile_tri(rows, L):
    i = np.arange(rows)[:, None]
    j = np.arange(rows)[None, :]
    same = (i // L) == (j // L)
    return (same & (j <= i)).astype(np.float32), same.astype(np.float32)


def _level_ref(G, b):
    L, W = G.shape
    span = 2 * b
    if span >= SUBLANES:
        return jnp.concatenate([jnp.broadcast_to(G[s + b - 1:s + b], (span, W)) for s in range(0, L, span)], 0)
    pos = (_iota2((L, 1), 0) & (SUBLANES - 1)) >> int(math.log2(span))
    ref = None
    for j in range(SUBLANES // span):
        cand = jnp.concatenate([jnp.broadcast_to(G[s + j * span + b - 1:s + j * span + b], (SUBLANES, W))
                                for s in range(0, L, SUBLANES)], 0)
        ref = cand if ref is None else jnp.where(pos == j, cand, ref)
    return ref


def _group_mats(width, group):
    i = np.arange(width)[:, None]
    j = np.arange(LANES)[None, :]
    ones = ((i // group) == j).astype(np.float32)
    return ones, np.concatenate([ones.T, ones.T], 0)


def _const(a):
    return jnp.asarray(a, BF16)


def _ffn_kernel(x_ref, wup_ref, wdn_ref, g_ref, b_ref, o_ref):
    x = x_ref[...]
    xb = x.astype(BF16)
    acc = jnp.zeros(x.shape, F32)
    for j in range(FFN_HIDDEN // FFN_TH):
        gate = jnp.dot(xb, wup_ref[:, j * FFN_TH:(j + 1) * FFN_TH], preferred_element_type=F32)
        up = jnp.dot(xb, wup_ref[:, FFN_HIDDEN + j * FFN_TH:FFN_HIDDEN + (j + 1) * FFN_TH],
                     preferred_element_type=F32)
        act = (gate * _sigmoid(gate) * up).astype(BF16)
        acc = acc + jnp.dot(act, wdn_ref[j * FFN_TH:(j + 1) * FFN_TH, :], preferred_element_type=F32)
    o_ref[...] = _layer_norm(ALPHA * x + acc, g_ref[...], b_ref[...])


def _resident(shape):
    nd = len(shape)
    return pl.BlockSpec(shape, lambda *_: (0,) * nd, pipeline_mode=pl.Buffered(1))


def _ffn_layer(x, w_up, w_down, ln_g, ln_b, tm):
    M = x.shape[0]
    return pl.pallas_call(
        _ffn_kernel,
        grid=(M // tm,),
        in_specs=[pl.BlockSpec((tm, D_MODEL), lambda i: (i, 0)),
                  _resident(w_up.shape), _resident(w_down.shape),
                  _resident(ln_g.shape), _resident(ln_b.shape)],
        out_specs=pl.BlockSpec((tm, D_MODEL), lambda i: (i, 0)),
        out_shape=jax.ShapeDtypeStruct((M, D_MODEL), F32),
        compiler_params=pltpu.CompilerParams(dimension_semantics=("arbitrary",),
                                             vmem_limit_bytes=VMEM_LIMIT),
        name="ffn_deepnorm",
    )(x, w_up, w_down, ln_g, ln_b)


def _gla_kernel(x_ref, s0_ref, wqkvr_ref, wgl_ref, wgu_ref, bg_ref, gng_ref, gnb_ref, wo_ref,
                lng_ref, lnb_ref, tri_ref, ones_ref, exp_ref,
                y_ref, s_ref, q_s, k_s, v_s, g_s, o_s, *, bb, tm, L):
    R = bb * tm
    cpt = tm // L
    nlev = int(math.log2(L))

    @pl.when(pl.program_id(1) == 0)
    def _():
        s_ref[...] = s0_ref[...]

    x = x_ref[...].reshape(R, D_MODEL)
    xb = x.astype(BF16)
    proj = jnp.dot(xb, wqkvr_ref[...], preferred_element_type=F32)
    q_s[...] = proj[:, :GLA_KD] * (GLA_DK ** -0.5)
    k_s[...] = proj[:, GLA_KD:2 * GLA_KD]
    v_s[...] = proj[:, 2 * GLA_KD:2 * GLA_KD + GLA_VD]
    r = proj[:, 2 * GLA_KD + GLA_VD:]
    gl = jnp.dot(xb, wgl_ref[...], preferred_element_type=F32)
    pre = _dot(gl, wgu_ref[...]) + bg_ref[...]
    g_s[...] = -_softplus(-pre) / GLA_TAU

    row = _iota2((L, L), 0)
    col = _iota2((L, L), 1)
    eye = row == col
    rowv = _iota2((L, 1), 0)
    pair_masks = []
    odd_rows = []
    for li in range(nlev):
        b = L >> (li + 1)
        pair_masks.append((row ^ col) < 2 * b)
        odd_rows.append((rowv & b) != 0)
    tri = tri_ref[...]

    def chunk(n, carry):
        b_idx = n // cpt
        rows = pl.ds(pl.multiple_of(n * L, L), L)
        q = q_s[rows, :]
        k = k_s[rows, :]
        v = v_s[rows, :]
        g = g_s[rows, :]
        G = _dot_sel_k3(tri, g)
        g_last = G[L - 1:L]
        e_last = jnp.exp(g_last)
        qg = q * jnp.exp(G)
        kr = k * jnp.exp(g_last - G)
        qt, kt = [], []
        for li in range(nlev):
            b = L >> (li + 1)
            if b == 1:
                e = jnp.where(odd_rows[li], g, 0.0)
            else:
                ref = _level_ref(G, b)
                e = jnp.where(odd_rows[li], G - ref, ref - G)
            f = jnp.exp(e)
            qt.append(jnp.where(odd_rows[li], q * f, 0.0).astype(BF16))
            kt.append(jnp.where(odd_rows[li], 0.0, k * f).astype(BF16))
        heads = range(GLA_HEADS)
        ks = [slice(h * GLA_DK, (h + 1) * GLA_DK) for h in heads]
        vs = [slice(h * GLA_DV, (h + 1) * GLA_DV) for h in heads]
        sts = [s_ref[b_idx, h] for h in heads]
        scores = [[_dot_nt(qt[li][:, ks[h]], kt[li][:, ks[h]]) for li in range(nlev)] for h in heads]
        inter = [_dot_nt(qg[:, ks[h]], sts[h]) for h in heads]
        upd = [_dot_tn(v[:, vs[h]], kr[:, ks[h]]) for h in heads]
        for h in heads:
            A = jnp.where(eye, jnp.sum(q[:, ks[h]] * k[:, ks[h]], -1, keepdims=True), 0.0)
            for li in range(nlev):
                A = A + jnp.where(pair_masks[li], scores[h][li], 0.0)
            o_s[rows, vs[h]] = inter[h] + _dot(A, v[:, vs[h]])
            s_ref[b_idx, h] = sts[h] * e_last[:, ks[h]] + upd[h]
        return carry

    lax.fori_loop(0, bb * cpt, chunk, 0)

    o = o_s[...]
    mu = _seg_sum(o, ones_ref[...], exp_ref[...]) * (1.0 / GLA_DV)
    oc = o - mu
    var = _seg_sum(oc * oc, ones_ref[...], exp_ref[...]) * (1.0 / GLA_DV)
    on = oc * lax.rsqrt(var + LN_EPS) * gng_ref[...] + gnb_ref[...]
    h_mix = _dot(on * (r * _sigmoid(r)), wo_ref[...])
    y_ref[...] = _layer_norm(ALPHA * x + h_mix, lng_ref[...], lnb_ref[...]).reshape(bb, tm, D_MODEL)


def _mixer_call(kernel, name, x, states, consts, out_state_shapes, scratch, bb, tm):
    B, T, _ = x.shape
    grid = (B // bb, T // tm)

    def state_spec(shape):
        nd = len(shape)
        return pl.BlockSpec((bb,) + tuple(shape[1:]), lambda b, t: (b,) + (0,) * (nd - 1))

    in_specs = ([pl.BlockSpec((bb, tm, D_MODEL), lambda b, t: (b, t, 0))]
                + [state_spec(s.shape) for s in states]
                + [_resident(c.shape) for c in consts])
    out_specs = ([pl.BlockSpec((bb, tm, D_MODEL), lambda b, t: (b, t, 0))]
                 + [state_spec(s) for s in out_state_shapes])
    out_shape = ([jax.ShapeDtypeStruct((B, T, D_MODEL), F32)]
                 + [jax.ShapeDtypeStruct(s, F32) for s in out_state_shapes])
    return pl.pallas_call(
        kernel, grid=grid, in_specs=in_specs, out_specs=out_specs, out_shape=out_shape,
        scratch_shapes=scratch,
        compiler_params=pltpu.CompilerParams(dimension_semantics=("arbitrary", "arbitrary"),
                                             vmem_limit_bytes=VMEM_LIMIT),
        name=name,
    )(x, *states, *consts)


def _gla_layer(x, s0, w, ln_g, ln_b, bb, tm, L):
    R = bb * tm
    ones_m, exp_m = _group_mats(GLA_VD, GLA_DV)
    consts = [w["wqkvr"], w["wgl"], w["wgu"], w["bg"], w["gng"], w["gnb"], w["wo"], ln_g, ln_b,
              _const(np.tile(_tile_tri(L, L)[0], (1, 3))), _const(ones_m), _const(exp_m)]
    scratch = [pltpu.VMEM((R, GLA_KD), F32), pltpu.VMEM((R, GLA_KD), F32), pltpu.VMEM((R, GLA_VD), F32),
               pltpu.VMEM((R, GLA_KD), F32), pltpu.VMEM((R, GLA_VD), F32)]
    kern = functools.partial(_gla_kernel, bb=bb, tm=tm, L=L)
    return _mixer_call(kern, "gla_mixer", x, [s0], consts, [s0.shape], scratch, bb, tm)


def _lo_half(shape):
    return _iota2(shape, 1) < shape[1] // 2


def _bd(z):
    lo = _lo_half(z.shape)
    zero = jnp.zeros_like(z)
    return jnp.concatenate([jnp.where(lo, z, zero), jnp.where(lo, zero, z)], 0)


def _dot3_bd_many(ys, z):
    M = ys[0].shape[0]
    n = len(ys)
    parts = [_split2(y) for y in ys]
    his = [h for h, _ in parts]
    zh, zl = _split2(z)
    hh = jnp.dot(jnp.concatenate(his + [l for _, l in parts], 0), _bd(zh), preferred_element_type=F32)
    hl = jnp.dot(jnp.concatenate(his, 0) if n > 1 else his[0], _bd(zl), preferred_element_type=F32)
    return [hh[i * M:(i + 1) * M] + hh[(n + i) * M:(n + i + 1) * M] + hl[i * M:(i + 1) * M] for i in range(n)]


def _dot3_bd(y, z):
    return _dot3_bd_many([y], z)[0]


def _tri_inverse_pairs(Ts, L):
    row = _iota2((L, 2 * L), 0)
    col = _iota2((L, 2 * L), 1) & (L - 1)
    eye = (row == col).astype(F32)
    bs = min(16, L)
    Ps = [jnp.where((row ^ col) < bs, T, 0.0) for T in Ts]
    Xs = [eye - P for P in Ps]
    if bs > 2:
        Ps = [_dot3_bd(P, P) for P in Ps]
    p = 2
    while p < bs:
        if 2 * p < bs:
            both = [_dot3_bd_many([X, P], P) for X, P in zip(Xs, Ps)]
            Xs = [X + xp for X, (xp, _) in zip(Xs, both)]
            Ps = [pp for _, pp in both]
        else:
            Xs = [X + _dot3_bd(X, P) for X, P in zip(Xs, Ps)]
        p *= 2
    s = bs
    while s < L:
        off = ((row ^ col) < 2 * s) & ((row ^ col) >= s)
        XC = [_dot3_bd(X, jnp.where(off, T, 0.0)) for X, T in zip(Xs, Ts)]
        Xs = [X - _dot3_bd(Y, X) for X, Y in zip(Xs, XC)]
        s *= 2
    return Xs


def _rwkv_kernel(x_ref, s0_ref, sh0_ref, mu_ref, wr_ref, wk_ref, wv_ref, wo_ref, w0_ref, w1_ref, w2_ref,
                 a0_ref, a1_ref, a2_ref, g1_ref, g2_ref, kk_ref, ka_ref, rk_ref, lxg_ref, lxb_ref,
                 lng_ref, lnb_ref, tri_ref, tot_ref, ones_ref, exp_ref,
                 y_ref, s_ref, sh_ref, xs_s, pa_s, xr_s, wy_s, yh_s, *, bb, tm, L):
    R = bb * tm
    cpt = tm // L
    H, N = RWKV_HEADS, RWKV_HEAD

    @pl.when(pl.program_id(1) == 0)
    def _():
        s_ref[...] = s0_ref[...]
        xs_s[:, SUBLANES - 1:SUBLANES, :] = sh0_ref[...]

    x3 = x_ref[...]
    xs_s[:, SUBLANES:SUBLANES + tm, :] = x3
    xprev = xs_s[:, SUBLANES - 1:SUBLANES - 1 + tm, :].reshape(R, D_MODEL)
    last = x3[:, tm - 1:tm, :]
    xs_s[:, SUBLANES - 1:SUBLANES, :] = last
    sh_ref[...] = last
    x = x3.reshape(R, D_MODEL)
    xx = xprev - x
    mu = mu_ref[...]

    def mix(c):
        return (x + xx * mu[c:c + 1, :]).astype(BF16)

    r = jnp.dot(mix(0), wr_ref[...], preferred_element_type=F32)
    k = jnp.dot(mix(2), wk_ref[...], preferred_element_type=F32)
    v = jnp.dot(mix(3), wv_ref[...], preferred_element_type=F32)
    wl = jnp.tanh(jnp.dot(mix(1), w1_ref[...], preferred_element_type=F32))
    w_log = -_softplus(-(w0_ref[...] + _dot(wl, w2_ref[...]))) - 0.5
    lw = -jnp.exp(w_log)
    al = jnp.dot(mix(4), a1_ref[...], preferred_element_type=F32)
    a = _sigmoid(a0_ref[...] + _dot(al, a2_ref[...]))
    gg = _dot(_sigmoid(jnp.dot(mix(5), g1_ref[...], preferred_element_type=F32)), g2_ref[...])

    ones_m = ones_ref[...]
    exp_m = exp_ref[...]
    kk = k * kk_ref[...]
    kk = kk * lax.rsqrt(_seg_sum(kk * kk, ones_m, exp_m) + 1e-12)
    k = k * (1.0 + (a - 1.0) * ka_ref[...])
    bvec = kk * a

    gam = _dot_sel(tri_ref[...], lw)
    gl = _dot_sel(tot_ref[...], lw)
    e_neg = jnp.exp(-gam)
    e_rem = jnp.exp(gl - gam)
    alpha = kk * jnp.exp(gam - lw)
    rho = r * jnp.exp(gam)
    beta = bvec * e_neg
    kap = k * e_neg
    kap_l = k * e_rem
    bet_l = -(bvec * e_rem)
    e_tot = jnp.exp(gl)
    PW = 2 * N
    NP = H // 2
    for p in range(NP):
        ps = slice(p * PW, (p + 1) * PW)
        for slot, arr in enumerate((alpha, rho, beta, kap, v, kap_l, bet_l, e_tot)):
            pa_s[p, slot] = arr[:, ps]

    rowp = _iota2((L, 2 * L), 0)
    colp = _iota2((L, 2 * L), 1) & (L - 1)
    strict = colp < rowp
    incl = colp <= rowp
    lo_state = _lo_half((N, PW))

    def chunk_rows(n):
        return pl.ds(pl.multiple_of(n * L, L), L)

    n_chunks = bb * cpt
    ca = 2 if n_chunks % 2 == 0 else 1

    def prepare(i, carry):
        items = [(chunk_rows(i * ca + c), p) for c in range(ca) for p in range(NP)]
        ARs = [jnp.concatenate([pa_s[p, 0, rows, :], pa_s[p, 1, rows, :]], 0) for rows, p in items]
        Ms = [_dot_nt(AR, jnp.concatenate([_bd(pa_s[p, 2, rows, :]), _bd(pa_s[p, 3, rows, :])], 0))
              for AR, (rows, p) in zip(ARs, items)]
        vbd = [_bd(pa_s[p, 4, rows, :].astype(BF16)) for rows, p in items]
        Xs = _tri_inverse_pairs([jnp.where(strict, M[:L, :2 * L], 0.0) for M in Ms], L)
        for (rows, p), M, X, vb in zip(items, Ms, Xs, vbd):
            xr_s[p, 0, rows, :] = X
            xr_s[p, 1, rows, :] = jnp.where(incl, M[L:, :2 * L], 0.0)
            wy_s[p, 0, rows, :] = _dot(jnp.where(strict, M[:L, 2 * L:], 0.0), vb)
            wy_s[p, 1, rows, :] = _dot(jnp.where(incl, M[L:, 2 * L:], 0.0), vb)
        return carry

    lax.fori_loop(0, n_chunks // ca, prepare, 0)

    def recur(c, carry):
        items = [(b, chunk_rows(b * cpt + c), p) for b in range(bb) for p in range(NP)]
        Ss = [s_ref[b, p] for b, _, p in items]
        W0s = [_dot_nt(jnp.concatenate([pa_s[p, 0, rows, :], pa_s[p, 1, rows, :]], 0), _bd(S))
               for (_, rows, p), S in zip(items, Ss)]
        Us = [_dot3_bd(xr_s[p, 0, rows, :], W0[:L] + wy_s[p, 0, rows, :])
              for (_, rows, p), W0 in zip(items, W0s)]
        for (b, rows, p), S, W0, U in zip(items, Ss, W0s, Us):
            yh_s[p, rows, :] = W0[L:] - _dot(xr_s[p, 1, rows, :], _bd(U.astype(BF16))) + wy_s[p, 1, rows, :]
            kb = jnp.concatenate([pa_s[p, 5, rows, :], pa_s[p, 6, rows, :]], 0)
            cross = _dot_tn(jnp.concatenate([pa_s[p, 4, rows, :], U], 0), kb)
            e_row = pa_s[p, 7, pl.ds(pl.multiple_of((b * cpt + c) * L, L), 1), :]
            s_ref[b, p] = S * e_row + jnp.where(lo_state, cross[:N], cross[N:])
        return carry

    lax.fori_loop(0, cpt, recur, 0)

    y = jnp.concatenate([yh_s[p] for p in range(NP)], -1)
    mu_y = _seg_sum(y, ones_m, exp_m) * (1.0 / N)
    yc = y - mu_y
    var_y = _seg_sum(yc * yc, ones_m, exp_m) * (1.0 / N)
    yn = yc * lax.rsqrt(var_y + RWKV_LN_EPS) * lxg_ref[...] + lxb_ref[...]
    bonus = _seg_sum(r * k * rk_ref[...], ones_m, exp_m) * v
    h_mix = _dot((yn + bonus) * gg, wo_ref[...])
    y_ref[...] = _layer_norm(ALPHA * x + h_mix, lng_ref[...], lnb_ref[...]).reshape(bb, tm, D_MODEL)


def _rwkv_layer(x, s0, sh0, w, ln_g, ln_b, bb, tm, L):
    R = bb * tm
    H, N = RWKV_HEADS, RWKV_HEAD
    tri, tot = _tile_tri(R, L)
    ones_m, exp_m = _group_mats(D_MODEL, N)
    consts = [w["mu"], w["wr"], w["wk"], w["wv"], w["wo"], w["w0"], w["w1"], w["w2"], w["a0"], w["a1"],
              w["a2"], w["g1"], w["g2"], w["kk"], w["ka"], w["rk"], w["lxg"], w["lxb"], ln_g, ln_b,
              _const(tri), _const(tot), _const(ones_m), _const(exp_m)]
    scratch = [pltpu.VMEM((bb, tm + SUBLANES, D_MODEL), F32),
               pltpu.VMEM((H // 2, 8, R, 2 * N), F32), pltpu.VMEM((H // 2, 2, R, 2 * L), F32),
               pltpu.VMEM((H // 2, 2, R, 2 * N), F32), pltpu.VMEM((H // 2, R, 2 * N), F32)]
    kern = functools.partial(_rwkv_kernel, bb=bb, tm=tm, L=L)
    return _mixer_call(kern, "rwkv_mixer", x, [s0, sh0], consts, [s0.shape, sh0.shape], scratch, bb, tm)


def _ssd_kernel(x_ref, s0_ref, cv0_ref, wz_ref, wxbc_ref, wdts_ref, cw_ref, cb_ref,
                dbs_ref, alx_ref, als_ref, dsk_ref, ng_ref, wo_ref, lng_ref, lnb_ref,
                tri_ref, eye_ref, hexp_ref, ones_ref, exp_ref,
                y_ref, s_ref, cv_ref, xs_s, xh_s, xdt_s, b_s, c_s, ax_s, as_s, yo_s, *, bb, tm, L):
    R = bb * tm
    cpt = tm // L
    W = SSD_CONV
    PAD = SUBLANES

    @pl.when(pl.program_id(1) == 0)
    def _():
        s_ref[...] = s0_ref[...]
        xs_s[:, PAD - (W - 1):PAD, :] = cv0_ref[...]

    x = x_ref[...].reshape(R, D_MODEL)
    xb = x.astype(BF16)
    xbc_raw = jnp.dot(xb, wxbc_ref[...], preferred_element_type=F32)
    xs_s[:, PAD:PAD + tm, :] = xbc_raw.reshape(bb, tm, SSD_CONV_DIM)
    cw = cw_ref[...]
    conv = cb_ref[...].reshape(1, 1, SSD_CONV_DIM)
    for wi in range(W):
        lo = PAD - (W - 1) + wi
        conv = conv + xs_s[:, lo:lo + tm, :] * cw[wi:wi + 1, :].reshape(1, 1, SSD_CONV_DIM)
    tail = xs_s[:, PAD + tm - (W - 1):PAD + tm, :]
    xs_s[:, PAD - (W - 1):PAD, :] = tail
    cv_ref[...] = tail
    conv = conv.reshape(R, SSD_CONV_DIM)
    xbc = conv * _sigmoid(conv)
    xh = xbc[:, :SSD_DI]
    dt_s = _softplus(jnp.dot(xb, wdts_ref[...], preferred_element_type=F32) + dbs_ref[...])
    dt_x = _dot_sel_r(dt_s, hexp_ref[...])
    xh_s[...] = xh
    xdt_s[...] = xh * dt_x
    b_s[...] = xbc[:, SSD_DI:SSD_DI + SSD_GN]
    c_s[...] = xbc[:, SSD_DI + SSD_GN:]
    ax_s[...] = dt_x * (-jnp.exp(alx_ref[...]))
    as_s[...] = dt_s * (-jnp.exp(als_ref[...]))

    row = _iota2((L, L), 0)
    col = _iota2((L, L), 1)
    causal = col <= row
    tri = tri_ref[...]
    eye = eye_ref[...]

    def chunk(n, carry):
        b_idx = n // cpt
        rows = pl.ds(pl.multiple_of(n * L, L), L)
        cum = _dot_sel_k3(tri, ax_s[rows, :])
        cum_t = _dot_sel_nt(eye, _dot_sel_k3(tri, as_s[rows, :]))
        last = cum[L - 1:L]
        e_cum = jnp.exp(cum)
        e_last = jnp.exp(last)
        xdt = xdt_s[rows, :]
        xsc = xdt * jnp.exp(last - cum)
        Bm = b_s[rows, :]
        Cm = c_s[rows, :]
        groups = range(SSD_GROUPS)
        ns = [slice(g * SSD_DSTATE, (g + 1) * SSD_DSTATE) for g in groups]
        gs = [slice(g * SSD_GW, (g + 1) * SSD_GW) for g in groups]
        sts = [s_ref[b_idx, g] for g in groups]
        CBs = [_dot_nt(Cm[:, ns[g]], Bm[:, ns[g]]) for g in groups]
        inter = [_dot(Cm[:, ns[g]], sts[g]) for g in groups]
        upd = [_dot_tn(Bm[:, ns[g]], xsc[:, gs[g]]) for g in groups]
        for g in groups:
            ys = []
            for hh in range(SSD_HPG):
                h = g * SSD_HPG + hh
                d = cum[:, h * SSD_HEADDIM:h * SSD_HEADDIM + L] - cum_t[h:h + 1, :]
                dec = jnp.exp(jnp.where(causal, d, -jnp.inf))
                ys.append(_dot(CBs[g] * dec, xdt[:, h * SSD_HEADDIM:(h + 1) * SSD_HEADDIM]))
            yo_s[rows, gs[g]] = jnp.concatenate(ys, -1) + inter[g] * e_cum[:, gs[g]]
            s_ref[b_idx, g] = sts[g] * e_last[:, gs[g]] + upd[g]
        return carry

    lax.fori_loop(0, bb * cpt, chunk, 0)

    z = jnp.dot(xb, wz_ref[...], preferred_element_type=F32)
    y = (yo_s[...] + dsk_ref[...] * xh_s[...]) * (z * _sigmoid(z))
    ms = _seg_sum(y * y, ones_ref[...], exp_ref[...]) * (1.0 / SSD_GW)
    yn = y * lax.rsqrt(ms + LN_EPS) * ng_ref[...]
    h_mix = _dot(yn, wo_ref[...])
    y_ref[...] = _layer_norm(ALPHA * x + h_mix, lng_ref[...], lnb_ref[...]).reshape(bb, tm, D_MODEL)


def _ssd_layer(x, s0, cv0, w, ln_g, ln_b, bb, tm, L):
    R = bb * tm
    tri, _ = _tile_tri(L, L)
    ones_m, exp_m = _group_mats(SSD_DI, SSD_GW)
    head_exp = (np.arange(LANES)[:, None] == np.arange(SSD_DI)[None, :] // SSD_HEADDIM).astype(np.float32)
    consts = [w["wz"], w["wxbc"], w["wdts"], w["cw"], w["cb"], w["dbs"], w["alx"],
              w["als"], w["dsk"], w["ng"], w["wo"], ln_g, ln_b,
              _const(np.tile(tri, (1, 3))), _const(np.eye(LANES, dtype=np.float32)),
              _const(np.tile(head_exp, (2, 1))),
              _const(ones_m), _const(exp_m)]
    scratch = [pltpu.VMEM((bb, tm + SUBLANES, SSD_CONV_DIM), F32),
               pltpu.VMEM((R, SSD_DI), F32), pltpu.VMEM((R, SSD_DI), F32),
               pltpu.VMEM((R, SSD_GN), F32), pltpu.VMEM((R, SSD_GN), F32),
               pltpu.VMEM((R, SSD_DI), F32), pltpu.VMEM((R, LANES), F32), pltpu.VMEM((R, SSD_DI), F32)]
    kern = functools.partial(_ssd_kernel, bb=bb, tm=tm, L=L)
    return _mixer_call(kern, "ssd_mixer", x, [s0, cv0], consts, [s0.shape, cv0.shape], scratch, bb, tm)


def _row(v):
    return v.reshape(1, -1).astype(F32)


def _pad_cols(w, n):
    return jnp.pad(w, ((0, 0), (0, n - w.shape[1])))


def _pad_rows(w, n):
    return jnp.pad(w, ((0, n - w.shape[0]), (0, 0)))


def _prep_gla(p, j):
    w_in = p["gla_w_in"][j]
    o_gl = 2 * GLA_KD + GLA_VD
    return {
        "wqkvr": jnp.concatenate([w_in[:, :o_gl], w_in[:, o_gl + GLA_GATE_RANK:]], 1).astype(BF16),
        "wgl": _pad_cols(w_in[:, o_gl:o_gl + GLA_GATE_RANK], LANES).astype(BF16),
        "wgu": _pad_rows(p["gla_w_gate_up"][j], LANES).astype(BF16),
        "bg": _row(p["gla_b_gate"][j]), "gng": _row(p["gla_gn_g"][j]), "gnb": _row(p["gla_gn_b"][j]),
        "wo": p["gla_w_out"][j].astype(BF16),
    }


def _prep_rwkv(p, j):
    return {
        "mu": p["rwkv_mu"][j].astype(F32),
        "wr": p["rwkv_w_rkv"][j, 0].astype(BF16), "wk": p["rwkv_w_rkv"][j, 1].astype(BF16),
        "wv": p["rwkv_w_rkv"][j, 2].astype(BF16), "wo": p["rwkv_w_o"][j].astype(BF16),
        "w0": _row(p["rwkv_w0"][j]),
        "w1": _pad_cols(p["rwkv_w1"][j], RWKV_LORA_PAD).astype(BF16),
        "w2": _pad_rows(p["rwkv_w2"][j], RWKV_LORA_PAD).astype(BF16),
        "a0": _row(p["rwkv_a0"][j]),
        "a1": _pad_cols(p["rwkv_a1"][j], RWKV_LORA_PAD).astype(BF16),
        "a2": _pad_rows(p["rwkv_a2"][j], RWKV_LORA_PAD).astype(BF16),
        "g1": p["rwkv_g1"][j].astype(BF16), "g2": p["rwkv_g2"][j].astype(BF16),
        "kk": _row(p["rwkv_k_k"][j]), "ka": _row(p["rwkv_k_a"][j]), "rk": _row(p["rwkv_r_k"][j]),
        "lxg": _row(p["rwkv_lnx_g"][j]), "lxb": _row(p["rwkv_lnx_b"][j]),
    }


def _prep_ssd(p, j):
    w_in = p["ssd_w_in"][j]
    w_dt = w_in[:, SSD_DI + SSD_CONV_DIM:]
    rep = lambda v: jnp.repeat(v, SSD_HEADDIM, axis=-1)
    return {
        "wz": w_in[:, :SSD_DI].astype(BF16),
        "wxbc": w_in[:, SSD_DI:SSD_DI + SSD_CONV_DIM].astype(BF16),
        "wdts": _pad_cols(w_dt, LANES).astype(BF16),
        "cw": p["ssd_conv_w"][j].astype(F32), "cb": _row(p["ssd_conv_b"][j]),
        "dbs": _pad_cols(_row(p["ssd_dt_bias"][j]), LANES),
        "alx": _row(rep(p["ssd_A_log"][j])), "als": _pad_cols(_row(p["ssd_A_log"][j]), LANES),
        "dsk": _row(rep(p["ssd_D"][j])), "ng": _row(p["ssd_norm_g"][j]),
        "wo": p["ssd_w_out"][j].astype(BF16),
    }


def _trunk(x, st, W, tiles, L, ffn_tm):
    B, T, _ = x.shape
    new = {"gla": [], "rwkv": [], "shift": [], "ssm": [], "conv": []}
    for i in range(DEPTH):
        j, kind = i // 3, i % 3
        lg, lb = W["ln_g"][i], W["ln_b"][i]
        bb, tm = tiles[kind]
        if kind == 0:
            x, s = _gla_layer(x, st["gla"][j], W["gla"][j], lg[0], lb[0], bb, tm, L)
            new["gla"].append(s)
        elif kind == 1:
            x, s, sh = _rwkv_layer(x, st["rwkv"][j], st["shift"][j], W["rwkv"][j], lg[0], lb[0], bb, tm, L)
            new["rwkv"].append(s)
            new["shift"].append(sh)
        else:
            x, s, c = _ssd_layer(x, st["ssm"][j], st["conv"][j], W["ssd"][j], lg[0], lb[0], bb, tm, L)
            new["ssm"].append(s)
            new["conv"].append(c)
        x = _ffn_layer(x.reshape(B * T, D_MODEL), W["ffn_up"][i], W["ffn_down"][i], lg[1], lb[1],
                       ffn_tm).reshape(B, T, D_MODEL)
    return x, new


def _rwkv_to_kernel(s):
    B = s.shape[0]
    s = s.reshape(B, RWKV_HEADS // 2, 2, RWKV_HEAD, RWKV_HEAD)
    return jnp.swapaxes(s, 2, 3).reshape(B, RWKV_HEADS // 2, RWKV_HEAD, 2 * RWKV_HEAD)


def _rwkv_from_kernel(s):
    B = s.shape[0]
    s = s.reshape(B, RWKV_HEADS // 2, RWKV_HEAD, 2, RWKV_HEAD)
    return jnp.swapaxes(s, 2, 3).reshape(B, RWKV_HEADS, RWKV_HEAD, RWKV_HEAD)


def _ssm_to_kernel(s):
    B = s.shape[0]
    return jnp.swapaxes(s.reshape(B, SSD_GROUPS, SSD_GW, SSD_DSTATE), -1, -2)


def _ssm_from_kernel(s):
    B = s.shape[0]
    return jnp.swapaxes(s, -1, -2).reshape(B, SSD_HEADS, SSD_HEADDIM, SSD_DSTATE)


def kernel(x_prompt, x_sample, state_gla, state_rwkv, state_shift, state_ssm, state_conv, meta_tokens, gla_w_in, gla_w_gate_up, gla_b_gate, gla_gn_g, gla_gn_b, gla_w_out, rwkv_mu, rwkv_w_rkv, rwkv_w_o, rwkv_w0, rwkv_w1, rwkv_w2, rwkv_a0, rwkv_a1, rwkv_a2, rwkv_g1, rwkv_g2, rwkv_k_k, rwkv_k_a, rwkv_r_k, rwkv_lnx_g, rwkv_lnx_b, ssd_w_in, ssd_conv_w, ssd_conv_b, ssd_dt_bias, ssd_A_log, ssd_D, ssd_norm_g, ssd_w_out, ffn_w_up, ffn_w_down, ln_g, ln_b):
    p = dict(gla_w_in=gla_w_in, gla_w_gate_up=gla_w_gate_up, gla_b_gate=gla_b_gate, gla_gn_g=gla_gn_g,
             gla_gn_b=gla_gn_b, gla_w_out=gla_w_out, rwkv_mu=rwkv_mu, rwkv_w_rkv=rwkv_w_rkv,
             rwkv_w_o=rwkv_w_o, rwkv_w0=rwkv_w0, rwkv_w1=rwkv_w1, rwkv_w2=rwkv_w2, rwkv_a0=rwkv_a0,
             rwkv_a1=rwkv_a1, rwkv_a2=rwkv_a2, rwkv_g1=rwkv_g1, rwkv_g2=rwkv_g2, rwkv_k_k=rwkv_k_k,
             rwkv_k_a=rwkv_k_a, rwkv_r_k=rwkv_r_k, rwkv_lnx_g=rwkv_lnx_g, rwkv_lnx_b=rwkv_lnx_b,
             ssd_w_in=ssd_w_in, ssd_conv_w=ssd_conv_w, ssd_conv_b=ssd_conv_b, ssd_dt_bias=ssd_dt_bias,
             ssd_A_log=ssd_A_log, ssd_D=ssd_D, ssd_norm_g=ssd_norm_g, ssd_w_out=ssd_w_out)
    n_gla, n_rwkv, n_ssd = gla_w_in.shape[0], rwkv_mu.shape[0], ssd_w_in.shape[0]
    W = {
        "gla": [_prep_gla(p, j) for j in range(n_gla)],
        "rwkv": [_prep_rwkv(p, j) for j in range(n_rwkv)],
        "ssd": [_prep_ssd(p, j) for j in range(n_ssd)],
        "ffn_up": [ffn_w_up[i].astype(BF16) for i in range(DEPTH)],
        "ffn_down": [ffn_w_down[i].astype(BF16) for i in range(DEPTH)],
        "ln_g": [[_row(ln_g[i, c]) for c in range(2)] for i in range(DEPTH)],
        "ln_b": [[_row(ln_b[i, c]) for c in range(2)] for i in range(DEPTH)],
    }
    Bp, Tp, _ = x_prompt.shape
    Bs, Ts, _ = x_sample.shape
    assert Ts == N_META and Tp % CHUNK == 0

    def with_meta(s):
        return jnp.concatenate([jnp.zeros((Bp,) + s.shape[1:], F32), s.astype(F32)], 0)

    x_small = jnp.concatenate(
        [jnp.broadcast_to(meta_tokens.astype(F32)[None], (Bp, N_META, D_MODEL)), x_sample.astype(F32)], 0)
    st_small = {
        "gla": [with_meta(jnp.swapaxes(state_gla[j], -1, -2)) for j in range(n_gla)],
        "rwkv": [with_meta(_rwkv_to_kernel(state_rwkv[j])) for j in range(n_rwkv)],
        "shift": [with_meta(state_shift[j]) for j in range(n_rwkv)],
        "ssm": [with_meta(_ssm_to_kernel(state_ssm[j])) for j in range(n_ssd)],
        "conv": [with_meta(state_conv[j]) for j in range(n_ssd)],
    }
    Bsm = Bp + Bs
    bb_small = 3 if Bsm % 3 == 0 else 1
    y_small, new_small = _trunk(x_small, st_small, W, [(bb_small, N_META)] * 3, N_META, Bsm * N_META)
    st_prompt = {k: [s[:Bp] for s in v] for k, v in new_small.items()}
    tm = 256 if Tp % 256 == 0 else CHUNK
    ffn_tm = 512 if (Bp * Tp) % 512 == 0 else CHUNK
    tiles = [(1, tm), (Bp, max(tm // Bp, CHUNK)), (1, tm)]
    y_prompt, new_prompt = _trunk(x_prompt.astype(F32), st_prompt, W, tiles, CHUNK, ffn_tm)

    def outs(new, sl):
        return (jnp.stack([jnp.swapaxes(s[sl], -1, -2) for s in new["gla"]]),
                jnp.stack([_rwkv_from_kernel(s[sl]) for s in new["rwkv"]]),
                jnp.stack([s[sl] for s in new["shift"]]),
                jnp.stack([_ssm_from_kernel(s[sl]) for s in new["ssm"]]),
                jnp.stack([s[sl] for s in new["conv"]]))

    return (y_prompt, y_small[Bp:]) + outs(new_prompt, slice(None)) + outs(new_small, slice(Bp, None))
```

```python
import functools
import math

import numpy as np
import jax
import jax.numpy as jnp
from jax import lax
from jax.experimental import pallas as pl
from jax.experimental.pallas import tpu as pltpu

F32 = jnp.float32
BF16 = jnp.bfloat16

D_MODEL = 1024
DEPTH = 4
CHUNK = 64
N_META = 16
ALPHA = (2.0 * DEPTH) ** 0.25
LN_EPS = 1e-5

GLA_HEADS = 4
GLA_KD = D_MODEL // 2
GLA_VD = D_MODEL
GLA_DK = GLA_KD // GLA_HEADS
GLA_DV = GLA_VD // GLA_HEADS
GLA_GATE_RANK = 16
GLA_TAU = 16.0

RWKV_HEAD = 64
RWKV_HEADS = D_MODEL // RWKV_HEAD
RWKV_LN_EPS = 64e-5
RWKV_LORA_PAD = 128

SSD_DI = 2 * D_MODEL
SSD_HEADDIM = 64
SSD_HEADS = SSD_DI // SSD_HEADDIM
SSD_GROUPS = 8
SSD_HPG = SSD_HEADS // SSD_GROUPS
SSD_DSTATE = 128
SSD_CONV = 4
SSD_GN = SSD_GROUPS * SSD_DSTATE
SSD_CONV_DIM = SSD_DI + 2 * SSD_GN
SSD_GW = SSD_HPG * SSD_HEADDIM

FFN_HIDDEN = ((8 * D_MODEL + 3 * 256 - 1) // (3 * 256)) * 256
FFN_TH = 256

LANES = 128
SUBLANES = 8
VMEM_LIMIT = 56 * 1024 * 1024


def _dot(a, b):
    return jnp.dot(a.astype(BF16), b.astype(BF16), preferred_element_type=F32)


def _dot_nt(a, b):
    return lax.dot_general(a.astype(BF16), b.astype(BF16), (((1,), (1,)), ((), ())),
                           preferred_element_type=F32)


def _dot_tn(a, b):
    return lax.dot_general(a.astype(BF16), b.astype(BF16), (((0,), (0,)), ((), ())),
                           preferred_element_type=F32)


def _split2(x):
    hi = x.astype(BF16)
    lo = (x - hi.astype(F32)).astype(BF16)
    return hi, lo


def _split3(x):
    hi = x.astype(BF16)
    r = x - hi.astype(F32)
    mid = r.astype(BF16)
    lo = (r - mid.astype(F32)).astype(BF16)
    return hi, mid, lo


def _dot_sel_k3(w3, x):
    return jnp.dot(w3, jnp.concatenate(_split3(x), 0), preferred_element_type=F32)


def _dot_sel_r(x, w2):
    hi, mid, lo = _split3(x)
    return (jnp.dot(jnp.concatenate([hi, mid], -1), w2, preferred_element_type=F32)
            + jnp.dot(lo, w2[:x.shape[1]], preferred_element_type=F32))


def _dot_sel_nt(w, x):
    dn = (((1,), (1,)), ((), ()))
    hi, mid, lo = _split3(x)
    return (lax.dot_general(w, hi, dn, preferred_element_type=F32)
            + lax.dot_general(w, mid, dn, preferred_element_type=F32)
            + lax.dot_general(w, lo, dn, preferred_element_type=F32))


def _dot3(a, b):
    ah, al = _split2(a)
    bh, bl = _split2(b)
    return (jnp.dot(ah, bh, preferred_element_type=F32) + jnp.dot(ah, bl, preferred_element_type=F32)
            + jnp.dot(al, bh, preferred_element_type=F32))


def _seg_sum(x, ones_m, exp_m):
    hi, lo = _split2(x)
    s = jnp.dot(hi, ones_m, preferred_element_type=F32) + jnp.dot(lo, ones_m, preferred_element_type=F32)
    return jnp.dot(jnp.concatenate(_split2(s), -1), exp_m, preferred_element_type=F32)


def _layer_norm(x, g, b, eps=LN_EPS):
    mu = jnp.mean(x, -1, keepdims=True)
    xc = x - mu
    var = jnp.mean(xc * xc, -1, keepdims=True)
    return xc * lax.rsqrt(var + eps) * g + b


def _sigmoid(x):
    return 0.5 + 0.5 * jnp.tanh(0.5 * x)


def _softplus(x):
    return jnp.maximum(x, 0.0) + jnp.log1p(jnp.exp(-jnp.abs(x)))


def _iota2(shape, dim):
    return lax.broadcasted_iota(jnp.int32, shape, dim)


def _tile_tri(rows, L):
    i = np.arange(rows)[:, None]
    j = np.arange(rows)[None, :]
    same = (i // L) == (j // L)
    return (same & (j <= i)).astype(np.float32), same.astype(np.float32)


def _level_ref(G, b):
    L, W = G.shape
    span = 2 * b
    if span >= SUBLANES:
        return jnp.concatenate([jnp.broadcast_to(G[s + b - 1:s + b], (span, W)) for s in range(0, L, span)], 0)
    pos = (_iota2((L, 1), 0) & (SUBLANES - 1)) >> int(math.log2(span))
    ref = None
    for j in range(SUBLANES // span):
        cand = jnp.concatenate([jnp.broadcast_to(G[s + j * span + b - 1:s + j * span + b], (SUBLANES, W))
                                for s in range(0, L, SUBLANES)], 0)
        ref = cand if ref is None else jnp.where(pos == j, cand, ref)
    return ref


def _group_mats(width, group):
    i = np.arange(width)[:, None]
    j = np.arange(LANES)[None, :]
    ones = ((i // group) == j).astype(np.float32)
    return ones, np.concatenate([ones.T, ones.T], 0)


def _const(a):
    return jnp.asarray(a, BF16)


def _ffn_kernel(x_ref, wup_ref, wdn_ref, g_ref, b_ref, o_ref):
    x = x_ref[...]
    xb = x.astype(BF16)
    acc = jnp.zeros(x.shape, F32)
    for j in range(FFN_HIDDEN // FFN_TH):
        gate = jnp.dot(xb, wup_ref[:, j * FFN_TH:(j + 1) * FFN_TH], preferred_element_type=F32)
        up = jnp.dot(xb, wup_ref[:, FFN_HIDDEN + j * FFN_TH:FFN_HIDDEN + (j + 1) * FFN_TH],
                     preferred_element_type=F32)
        act = (gate * _sigmoid(gate) * up).astype(BF16)
        acc = acc + jnp.dot(act, wdn_ref[j * FFN_TH:(j + 1) * FFN_TH, :], preferred_element_type=F32)
    o_ref[...] = _layer_norm(ALPHA * x + acc, g_ref[...], b_ref[...])


def _resident(shape):
    nd = len(shape)
    return pl.BlockSpec(shape, lambda *_: (0,) * nd, pipeline_mode=pl.Buffered(1))


def _ffn_layer(x, w_up, w_down, ln_g, ln_b, tm):
    M = x.shape[0]
    return pl.pallas_call(
        _ffn_kernel,
        grid=(M // tm,),
        in_specs=[pl.BlockSpec((tm, D_MODEL), lambda i: (i, 0)),
                  _resident(w_up.shape), _resident(w_down.shape),
                  _resident(ln_g.shape), _resident(ln_b.shape)],
        out_specs=pl.BlockSpec((tm, D_MODEL), lambda i: (i, 0)),
        out_shape=jax.ShapeDtypeStruct((M, D_MODEL), F32),
        compiler_params=pltpu.CompilerParams(dimension_semantics=("arbitrary",),
                                             vmem_limit_bytes=VMEM_LIMIT),
        name="ffn_deepnorm",
    )(x, w_up, w_down, ln_g, ln_b)


def _gla_kernel(x_ref, s0_ref, wqkvr_ref, wgl_ref, wgu_ref, bg_ref, gng_ref, gnb_ref, wo_ref,
                lng_ref, lnb_ref, tri_ref, ones_ref, exp_ref,
                y_ref, s_ref, q_s, k_s, v_s, g_s, o_s, *, bb, tm, L):
    R = bb * tm
    cpt = tm // L
    nlev = int(math.log2(L))

    @pl.when(pl.program_id(1) == 0)
    def _():
        s_ref[...] = s0_ref[...]

    x = x_ref[...].reshape(R, D_MODEL)
    xb = x.astype(BF16)
    proj = jnp.dot(xb, wqkvr_ref[...], preferred_element_type=F32)
    q_s[...] = proj[:, :GLA_KD] * (GLA_DK ** -0.5)
    k_s[...] = proj[:, GLA_KD:2 * GLA_KD]
    v_s[...] = proj[:, 2 * GLA_KD:2 * GLA_KD + GLA_VD]
    r = proj[:, 2 * GLA_KD + GLA_VD:]
    gl = jnp.dot(xb, wgl_ref[...], preferred_element_type=F32)
    pre = _dot(gl, wgu_ref[...]) + bg_ref[...]
    g_s[...] = -_softplus(-pre) / GLA_TAU

    row = _iota2((L, L), 0)
    col = _iota2((L, L), 1)
    eye = row == col
    rowv = _iota2((L, 1), 0)
    pair_masks = []
    odd_rows = []
    for li in range(nlev):
        b = L >> (li + 1)
        pair_masks.append(((row ^ col) < 2 * b) & ((row & b) != 0) & ((col & b) == 0))
        odd_rows.append((rowv & b) != 0)
    tri = tri_ref[...]

    heads = range(GLA_HEADS)
    ks = [slice(h * GLA_DK, (h + 1) * GLA_DK) for h in heads]
    vs = [slice(h * GLA_DV, (h + 1) * GLA_DV) for h in heads]
    n_chunks = bb * cpt
    ca = 2 if n_chunks % 2 == 0 else 1

    def chunks(i, carry):
        ns = [i * ca + c for c in range(ca)]
        rows = [pl.ds(pl.multiple_of(n * L, L), L) for n in ns]
        qs = [q_s[r, :] for r in rows]
        kk = [k_s[r, :] for r in rows]
        vv = [v_s[r, :] for r in rows]
        gs = [g_s[r, :] for r in rows]
        Gs = [_dot_sel_k3(tri, g) for g in gs]
        qks = []
        for q, k, g, G in zip(qs, kk, gs, Gs):
            qk = []
            for li in range(nlev):
                b = L >> (li + 1)
                if b == 1:
                    e = jnp.where(odd_rows[li], g, 0.0)
                else:
                    ref = _level_ref(G, b)
                    e = jnp.where(odd_rows[li], G - ref, ref - G)
                qk.append((jnp.where(odd_rows[li], q, k) * jnp.exp(e)).astype(BF16))
            qks.append(qk)
        scores = [[[_dot_nt(qk[li][:, ks[h]], qk[li][:, ks[h]]) for li in range(nlev)] for h in heads]
                  for qk in qks]
        upds = [[_dot_tn(v[:, vs[h]], (k * jnp.exp(G[L - 1:L] - G))[:, ks[h]]) for h in heads]
                for v, k, G in zip(vv, kk, Gs)]
        intras = []
        for q, k, v, sc in zip(qs, kk, vv, scores):
            per_head = []
            for h in heads:
                A = jnp.where(eye, jnp.sum(q[:, ks[h]] * k[:, ks[h]], -1, keepdims=True), 0.0)
                for li in range(nlev):
                    A = jnp.where(pair_masks[li], sc[h][li], A)
                per_head.append(_dot(A, v[:, vs[h]]))
            intras.append(per_head)
        for n, r, q, G, intra, upd in zip(ns, rows, qs, Gs, intras, upds):
            b_idx = n // cpt
            qg = q * jnp.exp(G)
            e_last = jnp.exp(G[L - 1:L])
            sts = [s_ref[b_idx, h] for h in heads]
            inter = [_dot_nt(qg[:, ks[h]], sts[h]) for h in heads]
            for h in heads:
                o_s[r, vs[h]] = inter[h] + intra[h]
                s_ref[b_idx, h] = sts[h] * e_last[:, ks[h]] + upd[h]
        return carry

    lax.fori_loop(0, n_chunks // ca, chunks, 0)

    o = o_s[...]
    mu = _seg_sum(o, ones_ref[...], exp_ref[...]) * (1.0 / GLA_DV)
    oc = o - mu
    var = _seg_sum(oc * oc, ones_ref[...], exp_ref[...]) * (1.0 / GLA_DV)
    on = oc * lax.rsqrt(var + LN_EPS) * gng_ref[...] + gnb_ref[...]
    h_mix = _dot(on * (r * _sigmoid(r)), wo_ref[...])
    y_ref[...] = _layer_norm(ALPHA * x + h_mix, lng_ref[...], lnb_ref[...]).reshape(bb, tm, D_MODEL)


def _mixer_call(kernel, name, x, states, consts, out_state_shapes, scratch, bb, tm):
    B, T, _ = x.shape
    grid = (B // bb, T // tm)

    def state_spec(shape):
        nd = len(shape)
        return pl.BlockSpec((bb,) + tuple(shape[1:]), lambda b, t: (b,) + (0,) * (nd - 1))

    in_specs = ([pl.BlockSpec((bb, tm, D_MODEL), lambda b, t: (b, t, 0))]
                + [state_spec(s.shape) for s in states]
                + [_resident(c.shape) for c in consts])
    out_specs = ([pl.BlockSpec((bb, tm, D_MODEL), lambda b, t: (b, t, 0))]
                 + [state_spec(s) for s in out_state_shapes])
    out_shape = ([jax.ShapeDtypeStruct((B, T, D_MODEL), F32)]
                 + [jax.ShapeDtypeStruct(s, F32) for s in out_state_shapes])
    return pl.pallas_call(
        kernel, grid=grid, in_specs=in_specs, out_specs=out_specs, out_shape=out_shape,
        scratch_shapes=scratch,
        compiler_params=pltpu.CompilerParams(dimension_semantics=("arbitrary", "arbitrary"),
                                             vmem_limit_bytes=VMEM_LIMIT),
        name=name,
    )(x, *states, *consts)


def _gla_layer(x, s0, w, ln_g, ln_b, bb, tm, L):
    R = bb * tm
    ones_m, exp_m = _group_mats(GLA_VD, GLA_DV)
    consts = [w["wqkvr"], w["wgl"], w["wgu"], w["bg"], w["gng"], w["gnb"], w["wo"], ln_g, ln_b,
              _const(np.tile(_tile_tri(L, L)[0], (1, 3))), _const(ones_m), _const(exp_m)]
    scratch = [pltpu.VMEM((R, GLA_KD), F32), pltpu.VMEM((R, GLA_KD), F32), pltpu.VMEM((R, GLA_VD), F32),
               pltpu.VMEM((R, GLA_KD), F32), pltpu.VMEM((R, GLA_VD), F32)]
    kern = functools.partial(_gla_kernel, bb=bb, tm=tm, L=L)
    return _mixer_call(kern, "gla_mixer", x, [s0], consts, [s0.shape], scratch, bb, tm)


def _lo_half(shape):
    return _iota2(shape, 1) < shape[1] // 2


def _bd(z):
    lo = _lo_half(z.shape)
    zero = jnp.zeros_like(z)
    return jnp.concatenate([jnp.where(lo, z, zero), jnp.where(lo, zero, z)], 0)


def _dot3_bd_many(ys, z):
    M = ys[0].shape[0]
    n = len(ys)
    parts = [_split2(y) for y in ys]
    his = [h for h, _ in parts]
    zh, zl = _split2(z)
    hh = jnp.dot(jnp.concatenate(his + [l for _, l in parts], 0), _bd(zh), preferred_element_type=F32)
    hl = jnp.dot(jnp.concatenate(his, 0) if n > 1 else his[0], _bd(zl), preferred_element_type=F32)
    return [hh[i * M:(i + 1) * M] + hh[(n + i) * M:(n + i + 1) * M] + hl[i * M:(i + 1) * M] for i in range(n)]


def _dot3_bd(y, z):
    return _dot3_bd_many([y], z)[0]


def _tri_inverse_pairs(Ts, L):
    row = _iota2((L, 2 * L), 0)
    col = _iota2((L, 2 * L), 1) & (L - 1)
    eye = (row == col).astype(F32)
    bs = min(16, L)
    Ps = [jnp.where((row ^ col) < bs, T, 0.0) for T in Ts]
    Xs = [eye - P for P in Ps]
    if bs > 2:
        Ps = [_dot3_bd(P, P) for P in Ps]
    p = 2
    while p < bs:
        if 2 * p < bs:
            both = [_dot3_bd_many([X, P], P) for X, P in zip(Xs, Ps)]
            Xs = [X + xp for X, (xp, _) in zip(Xs, both)]
            Ps = [pp for _, pp in both]
        else:
            Xs = [X + _dot3_bd(X, P) for X, P in zip(Xs, Ps)]
        p *= 2
    s = bs
    while s < L:
        off = ((row ^ col) < 2 * s) & ((row ^ col) >= s)
        XC = [_dot3_bd(X, jnp.where(off, T, 0.0)) for X, T in zip(Xs, Ts)]
        Xs = [X - _dot3_bd(Y, X) for X, Y in zip(Xs, XC)]
        s *= 2
    return Xs


def _rwkv_kernel(x_ref, s0_ref, sh0_ref, mu_ref, wr_ref, wk_ref, wv_ref, wo_ref, w0_ref, w1_ref, w2_ref,
                 a0_ref, a1_ref, a2_ref, g1_ref, g2_ref, kk_ref, ka_ref, rk_ref, lxg_ref, lxb_ref,
                 lng_ref, lnb_ref, tri_ref, ones_ref, exp_ref,
                 y_ref, s_ref, sh_ref, xs_s, pa_s, xr_s, wy_s, yh_s, *, bb, tm, L):
    R = bb * tm
    cpt = tm // L
    H, N = RWKV_HEADS, RWKV_HEAD

    @pl.when(pl.program_id(1) == 0)
    def _():
        s_ref[...] = s0_ref[...]
        xs_s[:, SUBLANES - 1:SUBLANES, :] = sh0_ref[...]

    x3 = x_ref[...]
    xs_s[:, SUBLANES:SUBLANES + tm, :] = x3
    xprev = xs_s[:, SUBLANES - 1:SUBLANES - 1 + tm, :].reshape(R, D_MODEL)
    last = x3[:, tm - 1:tm, :]
    xs_s[:, SUBLANES - 1:SUBLANES, :] = last
    sh_ref[...] = last
    x = x3.reshape(R, D_MODEL)
    xx = xprev - x
    mu = mu_ref[...]

    def mix(c):
        return (x + xx * mu[c:c + 1, :]).astype(BF16)

    r = jnp.dot(mix(0), wr_ref[...], preferred_element_type=F32)
    k = jnp.dot(mix(2), wk_ref[...], preferred_element_type=F32)
    v = jnp.dot(mix(3), wv_ref[...], preferred_element_type=F32)
    wl = jnp.tanh(jnp.dot(mix(1), w1_ref[...], preferred_element_type=F32))
    w_log = -_softplus(-(w0_ref[...] + _dot(wl, w2_ref[...]))) - 0.5
    lw = -jnp.exp(w_log)
    al = jnp.dot(mix(4), a1_ref[...], preferred_element_type=F32)
    a = _sigmoid(a0_ref[...] + _dot(al, a2_ref[...]))
    gg = _dot(_sigmoid(jnp.dot(mix(5), g1_ref[...], preferred_element_type=F32)), g2_ref[...])

    ones_m = ones_ref[...]
    exp_m = exp_ref[...]
    kk = k * kk_ref[...]
    kk = kk * lax.rsqrt(_seg_sum(kk * kk, ones_m, exp_m) + 1e-12)
    k = k * (1.0 + (a - 1.0) * ka_ref[...])
    bvec = kk * a

    tri3 = tri_ref[...]
    gams = [_dot_sel_k3(tri3, lw[c * L:(c + 1) * L]) for c in range(R // L)]
    gam = jnp.concatenate(gams, 0)
    gl = jnp.concatenate([jnp.broadcast_to(gc[L - 1:L], (L, D_MODEL)) for gc in gams], 0)
    e_neg = jnp.exp(-gam)
    e_rem = jnp.exp(gl - gam)
    alpha = kk * jnp.exp(gam - lw)
    rho = r * jnp.exp(gam)
    beta = bvec * e_neg
    kap = k * e_neg
    kap_l = k * e_rem
    bet_l = -(bvec * e_rem)
    e_tot = jnp.exp(gl)
    PW = 2 * N
    NP = H // 2
    for p in range(NP):
        ps = slice(p * PW, (p + 1) * PW)
        for slot, arr in enumerate((alpha, rho, beta, kap, v, kap_l, bet_l, e_tot)):
            pa_s[p, slot] = arr[:, ps]

    rowp = _iota2((L, 2 * L), 0)
    colp = _iota2((L, 2 * L), 1) & (L - 1)
    strict = colp < rowp
    incl = colp <= rowp
    lo_state = _lo_half((N, PW))

    def chunk_rows(n):
        return pl.ds(pl.multiple_of(n * L, L), L)

    n_chunks = bb * cpt
    ca = 2 if n_chunks % 2 == 0 else 1

    def prepare(i, carry):
        items = [(chunk_rows(i * ca + c), p) for c in range(ca) for p in range(NP)]
        ARs = [jnp.concatenate([pa_s[p, 0, rows, :], pa_s[p, 1, rows, :]], 0) for rows, p in items]
        Ms = [_dot_nt(AR, jnp.concatenate([_bd(pa_s[p, 2, rows, :]), _bd(pa_s[p, 3, rows, :])], 0))
              for AR, (rows, p) in zip(ARs, items)]
        vbd = [_bd(pa_s[p, 4, rows, :].astype(BF16)) for rows, p in items]
        Xs = _tri_inverse_pairs([jnp.where(strict, M[:L, :2 * L], 0.0) for M in Ms], L)
        for (rows, p), M, X, vb in zip(items, Ms, Xs, vbd):
            xr_s[p, 0, rows, :] = X
            xr_s[p, 1, rows, :] = jnp.where(incl, M[L:, :2 * L], 0.0)
            wy_s[p, 0, rows, :] = _dot(jnp.where(strict, M[:L, 2 * L:], 0.0), vb)
            wy_s[p, 1, rows, :] = _dot(jnp.where(incl, M[L:, 2 * L:], 0.0), vb)
        return carry

    lax.fori_loop(0, n_chunks // ca, prepare, 0)

    def recur(c, carry):
        items = [(b, chunk_rows(b * cpt + c), p) for b in range(bb) for p in range(NP)]
        Ss = [s_ref[b, p] for b, _, p in items]
        W0s = [_dot_nt(jnp.concatenate([pa_s[p, 0, rows, :], pa_s[p, 1, rows, :]], 0), _bd(S))
               for (_, rows, p), S in zip(items, Ss)]
        Us = [_dot3_bd(xr_s[p, 0, rows, :], W0[:L] + wy_s[p, 0, rows, :])
              for (_, rows, p), W0 in zip(items, W0s)]
        for (b, rows, p), S, W0, U in zip(items, Ss, W0s, Us):
            yh_s[p, rows, :] = W0[L:] - _dot(xr_s[p, 1, rows, :], _bd(U.astype(BF16))) + wy_s[p, 1, rows, :]
            kb = jnp.concatenate([pa_s[p, 5, rows, :], pa_s[p, 6, rows, :]], 0)
            cross = _dot_tn(jnp.concatenate([pa_s[p, 4, rows, :], U], 0), kb)
            e_row = pa_s[p, 7, pl.ds(pl.multiple_of((b * cpt + c) * L, L), 1), :]
            s_ref[b, p] = S * e_row + jnp.where(lo_state, cross[:N], cross[N:])
        return carry

    lax.fori_loop(0, cpt, recur, 0)

    y = jnp.concatenate([yh_s[p] for p in range(NP)], -1)
    mu_y = _seg_sum(y, ones_m, exp_m) * (1.0 / N)
    yc = y - mu_y
    var_y = _seg_sum(yc * yc, ones_m, exp_m) * (1.0 / N)
    yn = yc * lax.rsqrt(var_y + RWKV_LN_EPS) * lxg_ref[...] + lxb_ref[...]
    bonus = _seg_sum(r * k * rk_ref[...], ones_m, exp_m) * v
    h_mix = _dot((yn + bonus) * gg, wo_ref[...])
    y_ref[...] = _layer_norm(ALPHA * x + h_mix, lng_ref[...], lnb_ref[...]).reshape(bb, tm, D_MODEL)


def _rwkv_layer(x, s0, sh0, w, ln_g, ln_b, bb, tm, L):
    R = bb * tm
    H, N = RWKV_HEADS, RWKV_HEAD
    tri = np.tile(_tile_tri(L, L)[0], (1, 3))
    ones_m, exp_m = _group_mats(D_MODEL, N)
    consts = [w["mu"], w["wr"], w["wk"], w["wv"], w["wo"], w["w0"], w["w1"], w["w2"], w["a0"], w["a1"],
              w["a2"], w["g1"], w["g2"], w["kk"], w["ka"], w["rk"], w["lxg"], w["lxb"], ln_g, ln_b,
              _const(tri), _const(ones_m), _const(exp_m)]
    scratch = [pltpu.VMEM((bb, tm + SUBLANES, D_MODEL), F32),
               pltpu.VMEM((H // 2, 8, R, 2 * N), F32), pltpu.VMEM((H // 2, 2, R, 2 * L), F32),
               pltpu.VMEM((H // 2, 2, R, 2 * N), F32), pltpu.VMEM((H // 2, R, 2 * N), F32)]
    kern = functools.partial(_rwkv_kernel, bb=bb, tm=tm, L=L)
    return _mixer_call(kern, "rwkv_mixer", x, [s0, sh0], consts, [s0.shape, sh0.shape], scratch, bb, tm)


def _ssd_kernel(x_ref, s0_ref, cv0_ref, wz_ref, wxbc_ref, wdts_ref, cw_ref, cb_ref,
                dbs_ref, alx_ref, als_ref, dsk_ref, ng_ref, wo_ref, lng_ref, lnb_ref,
                tri_ref, eye_ref, hexp_ref, ones_ref, exp_ref,
                y_ref, s_ref, cv_ref, xs_s, xh_s, xdt_s, b_s, c_s, ax_s, as_s, yo_s, *, bb, tm, L):
    R = bb * tm
    cpt = tm // L
    W = SSD_CONV
    PAD = SUBLANES

    @pl.when(pl.program_id(1) == 0)
    def _():
        s_ref[...] = s0_ref[...]
        xs_s[:, PAD - (W - 1):PAD, :] = cv0_ref[...]

    x = x_ref[...].reshape(R, D_MODEL)
    xb = x.astype(BF16)
    xbc_raw = jnp.dot(xb, wxbc_ref[...], preferred_element_type=F32)
    xs_s[:, PAD:PAD + tm, :] = xbc_raw.reshape(bb, tm, SSD_CONV_DIM)
    cw = cw_ref[...]
    conv = cb_ref[...].reshape(1, 1, SSD_CONV_DIM)
    for wi in range(W):
        lo = PAD - (W - 1) + wi
        conv = conv + xs_s[:, lo:lo + tm, :] * cw[wi:wi + 1, :].reshape(1, 1, SSD_CONV_DIM)
    tail = xs_s[:, PAD + tm - (W - 1):PAD + tm, :]
    xs_s[:, PAD - (W - 1):PAD, :] = tail
    cv_ref[...] = tail
    conv = conv.reshape(R, SSD_CONV_DIM)
    xbc = conv * _sigmoid(conv)
    xh = xbc[:, :SSD_DI]
    dt_s = _softplus(jnp.dot(xb, wdts_ref[...], preferred_element_type=F32) + dbs_ref[...])
    dt_x = _dot_sel_r(dt_s, hexp_ref[...])
    xh_s[...] = xh
    xdt_s[...] = xh * dt_x
    b_s[...] = xbc[:, SSD_DI:SSD_DI + SSD_GN]
    c_s[...] = xbc[:, SSD_DI + SSD_GN:]
    ax_s[...] = dt_x * (-jnp.exp(alx_ref[...]))
    as_s[...] = dt_s * (-jnp.exp(als_ref[...]))

    row = _iota2((L, L), 0)
    col = _iota2((L, L), 1)
    causal = col <= row
    tri = tri_ref[...]
    eye = eye_ref[...]

    groups = range(SSD_GROUPS)
    ns = [slice(g * SSD_DSTATE, (g + 1) * SSD_DSTATE) for g in groups]
    gs = [slice(g * SSD_GW, (g + 1) * SSD_GW) for g in groups]
    n_chunks = bb * cpt
    ca = 2 if n_chunks % 2 == 0 else 1

    def chunks(i, carry):
        nn = [i * ca + c for c in range(ca)]
        rows = [pl.ds(pl.multiple_of(n * L, L), L) for n in nn]
        cums = [_dot_sel_k3(tri, ax_s[r, :]) for r in rows]
        cum_ts = [_dot_sel_nt(eye, _dot_sel_k3(tri, as_s[r, :])) for r in rows]
        xdts = [xdt_s[r, :] for r in rows]
        Bms = [b_s[r, :] for r in rows]
        Cms = [c_s[r, :] for r in rows]
        CBs = [[_dot_nt(Cm[:, ns[g]], Bm[:, ns[g]]) for g in groups] for Cm, Bm in zip(Cms, Bms)]
        upds = [[_dot_tn(Bm[:, ns[g]], (xdt * jnp.exp(cum[L - 1:L] - cum))[:, gs[g]]) for g in groups]
                for Bm, xdt, cum in zip(Bms, xdts, cums)]
        intras = []
        for cum, cum_t, xdt, CB in zip(cums, cum_ts, xdts, CBs):
            per_group = []
            for g in groups:
                ys = []
                for hh in range(SSD_HPG):
                    h = g * SSD_HPG + hh
                    d = cum[:, h * SSD_HEADDIM:h * SSD_HEADDIM + L] - cum_t[h:h + 1, :]
                    dec = jnp.exp(jnp.where(causal, d, -jnp.inf))
                    ys.append(_dot(CB[g] * dec, xdt[:, h * SSD_HEADDIM:(h + 1) * SSD_HEADDIM]))
                per_group.append(jnp.concatenate(ys, -1))
            intras.append(per_group)
        for n, r, cum, Cm, intra, upd in zip(nn, rows, cums, Cms, intras, upds):
            b_idx = n // cpt
            e_cum = jnp.exp(cum)
            e_last = jnp.exp(cum[L - 1:L])
            sts = [s_ref[b_idx, g] for g in groups]
            inter = [_dot(Cm[:, ns[g]], sts[g]) for g in groups]
            for g in groups:
                yo_s[r, gs[g]] = intra[g] + inter[g] * e_cum[:, gs[g]]
                s_ref[b_idx, g] = sts[g] * e_last[:, gs[g]] + upd[g]
        return carry

    lax.fori_loop(0, n_chunks // ca, chunks, 0)

    z = jnp.dot(xb, wz_ref[...], preferred_element_type=F32)
    y = (yo_s[...] + dsk_ref[...] * xh_s[...]) * (z * _sigmoid(z))
    ms = _seg_sum(y * y, ones_ref[...], exp_ref[...]) * (1.0 / SSD_GW)
    yn = y * lax.rsqrt(ms + LN_EPS) * ng_ref[...]
    h_mix = _dot(yn, wo_ref[...])
    y_ref[...] = _layer_norm(ALPHA * x + h_mix, lng_ref[...], lnb_ref[...]).reshape(bb, tm, D_MODEL)


def _ssd_layer(x, s0, cv0, w, ln_g, ln_b, bb, tm, L):
    R = bb * tm
    tri, _ = _tile_tri(L, L)
    ones_m, exp_m = _group_mats(SSD_DI, SSD_GW)
    head_exp = (np.arange(LANES)[:, None] == np.arange(SSD_DI)[None, :] // SSD_HEADDIM).astype(np.float32)
    consts = [w["wz"], w["wxbc"], w["wdts"], w["cw"], w["cb"], w["dbs"], w["alx"],
              w["als"], w["dsk"], w["ng"], w["wo"], ln_g, ln_b,
              _const(np.tile(tri, (1, 3))), _const(np.eye(LANES, dtype=np.float32)),
              _const(np.tile(head_exp, (2, 1))),
              _const(ones_m), _const(exp_m)]
    scratch = [pltpu.VMEM((bb, tm + SUBLANES, SSD_CONV_DIM), F32),
               pltpu.VMEM((R, SSD_DI), F32), pltpu.VMEM((R, SSD_DI), F32),
               pltpu.VMEM((R, SSD_GN), F32), pltpu.VMEM((R, SSD_GN), F32),
               pltpu.VMEM((R, SSD_DI), F32), pltpu.VMEM((R, LANES), F32), pltpu.VMEM((R, SSD_DI), F32)]
    kern = functools.partial(_ssd_kernel, bb=bb, tm=tm, L=L)
    return _mixer_call(kern, "ssd_mixer", x, [s0, cv0], consts, [s0.shape, cv0.shape], scratch, bb, tm)


def _row(v):
    return v.reshape(1, -1).astype(F32)


def _pad_cols(w, n):
    return jnp.pad(w, ((0, 0), (0, n - w.shape[1])))


def _pad_rows(w, n):
    return jnp.pad(w, ((0, n - w.shape[0]), (0, 0)))


def _prep_gla(p, j):
    w_in = p["gla_w_in"][j]
    o_gl = 2 * GLA_KD + GLA_VD
    return {
        "wqkvr": jnp.concatenate([w_in[:, :o_gl], w_in[:, o_gl + GLA_GATE_RANK:]], 1).astype(BF16),
        "wgl": _pad_cols(w_in[:, o_gl:o_gl + GLA_GATE_RANK], LANES).astype(BF16),
        "wgu": _pad_rows(p["gla_w_gate_up"][j], LANES).astype(BF16),
        "bg": _row(p["gla_b_gate"][j]), "gng": _row(p["gla_gn_g"][j]), "gnb": _row(p["gla_gn_b"][j]),
        "wo": p["gla_w_out"][j].astype(BF16),
    }


def _prep_rwkv(p, j):
    return {
        "mu": p["rwkv_mu"][j].astype(F32),
        "wr": p["rwkv_w_rkv"][j, 0].astype(BF16), "wk": p["rwkv_w_rkv"][j, 1].astype(BF16),
        "wv": p["rwkv_w_rkv"][j, 2].astype(BF16), "wo": p["rwkv_w_o"][j].astype(BF16),
        "w0": _row(p["rwkv_w0"][j]),
        "w1": _pad_cols(p["rwkv_w1"][j], RWKV_LORA_PAD).astype(BF16),
        "w2": _pad_rows(p["rwkv_w2"][j], RWKV_LORA_PAD).astype(BF16),
        "a0": _row(p["rwkv_a0"][j]),
        "a1": _pad_cols(p["rwkv_a1"][j], RWKV_LORA_PAD).astype(BF16),
        "a2": _pad_rows(p["rwkv_a2"][j], RWKV_LORA_PAD).astype(BF16),
        "g1": p["rwkv_g1"][j].astype(BF16), "g2": p["rwkv_g2"][j].astype(BF16),
        "kk": _row(p["rwkv_k_k"][j]), "ka": _row(p["rwkv_k_a"][j]), "rk": _row(p["rwkv_r_k"][j]),
        "lxg": _row(p["rwkv_lnx_g"][j]), "lxb": _row(p["rwkv_lnx_b"][j]),
    }


def _prep_ssd(p, j):
    w_in = p["ssd_w_in"][j]
    w_dt = w_in[:, SSD_DI + SSD_CONV_DIM:]
    rep = lambda v: jnp.repeat(v, SSD_HEADDIM, axis=-1)
    return {
        "wz": w_in[:, :SSD_DI].astype(BF16),
        "wxbc": w_in[:, SSD_DI:SSD_DI + SSD_CONV_DIM].astype(BF16),
        "wdts": _pad_cols(w_dt, LANES).astype(BF16),
        "cw": p["ssd_conv_w"][j].astype(F32), "cb": _row(p["ssd_conv_b"][j]),
        "dbs": _pad_cols(_row(p["ssd_dt_bias"][j]), LANES),
        "alx": _row(rep(p["ssd_A_log"][j])), "als": _pad_cols(_row(p["ssd_A_log"][j]), LANES),
        "dsk": _row(rep(p["ssd_D"][j])), "ng": _row(p["ssd_norm_g"][j]),
        "wo": p["ssd_w_out"][j].astype(BF16),
    }


def _trunk(x, st, W, tiles, L, ffn_tm):
    B, T, _ = x.shape
    new = {"gla": [], "rwkv": [], "shift": [], "ssm": [], "conv": []}
    for i in range(DEPTH):
        j, kind = i // 3, i % 3
        lg, lb = W["ln_g"][i], W["ln_b"][i]
        bb, tm = tiles[kind]
        if kind == 0:
            x, s = _gla_layer(x, st["gla"][j], W["gla"][j], lg[0], lb[0], bb, tm, L)
            new["gla"].append(s)
        elif kind == 1:
            x, s, sh = _rwkv_layer(x, st["rwkv"][j], st["shift"][j], W["rwkv"][j], lg[0], lb[0], bb, tm, L)
            new["rwkv"].append(s)
            new["shift"].append(sh)
        else:
            x, s, c = _ssd_layer(x, st["ssm"][j], st["conv"][j], W["ssd"][j], lg[0], lb[0], bb, tm, L)
            new["ssm"].append(s)
            new["conv"].append(c)
        x = _ffn_layer(x.reshape(B * T, D_MODEL), W["ffn_up"][i], W["ffn_down"][i], lg[1], lb[1],
                       ffn_tm).reshape(B, T, D_MODEL)
    return x, new


def _rwkv_to_kernel(s):
    B = s.shape[0]
    s = s.reshape(B, RWKV_HEADS // 2, 2, RWKV_HEAD, RWKV_HEAD)
    return jnp.swapaxes(s, 2, 3).reshape(B, RWKV_HEADS // 2, RWKV_HEAD, 2 * RWKV_HEAD)


def _rwkv_from_kernel(s):
    B = s.shape[0]
    s = s.reshape(B, RWKV_HEADS // 2, RWKV_HEAD, 2, RWKV_HEAD)
    return jnp.swapaxes(s, 2, 3).reshape(B, RWKV_HEADS, RWKV_HEAD, RWKV_HEAD)


def _ssm_to_kernel(s):
    B = s.shape[0]
    return jnp.swapaxes(s.reshape(B, SSD_GROUPS, SSD_GW, SSD_DSTATE), -1, -2)


def _ssm_from_kernel(s):
    B = s.shape[0]
    return jnp.swapaxes(s, -1, -2).reshape(B, SSD_HEADS, SSD_HEADDIM, SSD_DSTATE)


def kernel(x_prompt, x_sample, state_gla, state_rwkv, state_shift, state_ssm, state_conv, meta_tokens, gla_w_in, gla_w_gate_up, gla_b_gate, gla_gn_g, gla_gn_b, gla_w_out, rwkv_mu, rwkv_w_rkv, rwkv_w_o, rwkv_w0, rwkv_w1, rwkv_w2, rwkv_a0, rwkv_a1, rwkv_a2, rwkv_g1, rwkv_g2, rwkv_k_k, rwkv_k_a, rwkv_r_k, rwkv_lnx_g, rwkv_lnx_b, ssd_w_in, ssd_conv_w, ssd_conv_b, ssd_dt_bias, ssd_A_log, ssd_D, ssd_norm_g, ssd_w_out, ffn_w_up, ffn_w_down, ln_g, ln_b):
    p = dict(gla_w_in=gla_w_in, gla_w_gate_up=gla_w_gate_up, gla_b_gate=gla_b_gate, gla_gn_g=gla_gn_g,
             gla_gn_b=gla_gn_b, gla_w_out=gla_w_out, rwkv_mu=rwkv_mu, rwkv_w_rkv=rwkv_w_rkv,
             rwkv_w_o=rwkv_w_o, rwkv_w0=rwkv_w0, rwkv_w1=rwkv_w1, rwkv_w2=rwkv_w2, rwkv_a0=rwkv_a0,
             rwkv_a1=rwkv_a1, rwkv_a2=rwkv_a2, rwkv_g1=rwkv_g1, rwkv_g2=rwkv_g2, rwkv_k_k=rwkv_k_k,
             rwkv_k_a=rwkv_k_a, rwkv_r_k=rwkv_r_k, rwkv_lnx_g=rwkv_lnx_g, rwkv_lnx_b=rwkv_lnx_b,
             ssd_w_in=ssd_w_in, ssd_conv_w=ssd_conv_w, ssd_conv_b=ssd_conv_b, ssd_dt_bias=ssd_dt_bias,
             ssd_A_log=ssd_A_log, ssd_D=ssd_D, ssd_norm_g=ssd_norm_g, ssd_w_out=ssd_w_out)
    n_gla, n_rwkv, n_ssd = gla_w_in.shape[0], rwkv_mu.shape[0], ssd_w_in.shape[0]
    W = {
        "gla": [_prep_gla(p, j) for j in range(n_gla)],
        "rwkv": [_prep_rwkv(p, j) for j in range(n_rwkv)],
        "ssd": [_prep_ssd(p, j) for j in range(n_ssd)],
        "ffn_up": [ffn_w_up[i].astype(BF16) for i in range(DEPTH)],
        "ffn_down": [ffn_w_down[i].astype(BF16) for i in range(DEPTH)],
        "ln_g": [[_row(ln_g[i, c]) for c in range(2)] for i in range(DEPTH)],
        "ln_b": [[_row(ln_b[i, c]) for c in range(2)] for i in range(DEPTH)],
    }
    Bp, Tp, _ = x_prompt.shape
    Bs, Ts, _ = x_sample.shape
    assert Ts == N_META and Tp % CHUNK == 0

    def with_meta(s):
        return jnp.concatenate([jnp.zeros((Bp,) + s.shape[1:], F32), s.astype(F32)], 0)

    x_small = jnp.concatenate(
        [jnp.broadcast_to(meta_tokens.astype(F32)[None], (Bp, N_META, D_MODEL)), x_sample.astype(F32)], 0)
    st_small = {
        "gla": [with_meta(jnp.swapaxes(state_gla[j], -1, -2)) for j in range(n_gla)],
        "rwkv": [with_meta(_rwkv_to_kernel(state_rwkv[j])) for j in range(n_rwkv)],
        "shift": [with_meta(state_shift[j]) for j in range(n_rwkv)],
        "ssm": [with_meta(_ssm_to_kernel(state_ssm[j])) for j in range(n_ssd)],
        "conv": [with_meta(state_conv[j]) for j in range(n_ssd)],
    }
    Bsm = Bp + Bs
    bb_small = next(c for c in (6, 3, 2, 1) if Bsm % c == 0)
    y_small, new_small = _trunk(x_small, st_small, W, [(bb_small, N_META)] * 3, N_META, Bsm * N_META)
    st_prompt = {k: [s[:Bp] for s in v] for k, v in new_small.items()}
    tm = 256 if Tp % 256 == 0 else CHUNK
    ffn_tm = 512 if (Bp * Tp) % 512 == 0 else CHUNK
    tiles = [(1, tm), (Bp, max(tm // Bp, CHUNK)), (1, tm)]
    y_prompt, new_prompt = _trunk(x_prompt.astype(F32), st_prompt, W, tiles, CHUNK, ffn_tm)

    def outs(new, sl):
        return (jnp.stack([jnp.swapaxes(s[sl], -1, -2) for s in new["gla"]]),
                jnp.stack([_rwkv_from_kernel(s[sl]) for s in new["rwkv"]]),
                jnp.stack([s[sl] for s in new["shift"]]),
                jnp.stack([_ssm_from_kernel(s[sl]) for s in new["ssm"]]),
                jnp.stack([s[sl] for s in new["conv"]]))

    return (y_prompt, y_small[Bp:]) + outs(new_prompt, slice(None)) + outs(new_small, slice(Bp, None))
```

```python
import functools
import math

import numpy as np
import jax
import jax.numpy as jnp
from jax import lax
from jax.experimental import pallas as pl
from jax.experimental.pallas import tpu as pltpu

F32 = jnp.float32
BF16 = jnp.bfloat16

D_MODEL = 1024
DEPTH = 4
CHUNK = 64
N_META = 16
ALPHA = (2.0 * DEPTH) ** 0.25
LN_EPS = 1e-5

GLA_HEADS = 4
GLA_KD = D_MODEL // 2
GLA_VD = D_MODEL
GLA_DK = GLA_KD // GLA_HEADS
GLA_DV = GLA_VD // GLA_HEADS
GLA_GATE_RANK = 16
GLA_TAU = 16.0

RWKV_HEAD = 64
RWKV_HEADS = D_MODEL // RWKV_HEAD
RWKV_LN_EPS = 64e-5
RWKV_LORA_PAD = 128

SSD_DI = 2 * D_MODEL
SSD_HEADDIM = 64
SSD_HEADS = SSD_DI // SSD_HEADDIM
SSD_GROUPS = 8
SSD_HPG = SSD_HEADS // SSD_GROUPS
SSD_DSTATE = 128
SSD_CONV = 4
SSD_GN = SSD_GROUPS * SSD_DSTATE
SSD_CONV_DIM = SSD_DI + 2 * SSD_GN
SSD_GW = SSD_HPG * SSD_HEADDIM

FFN_HIDDEN = ((8 * D_MODEL + 3 * 256 - 1) // (3 * 256)) * 256
FFN_TH = 256

LANES = 128
SUBLANES = 8
VMEM_LIMIT = 56 * 1024 * 1024


def _dot(a, b):
    return jnp.dot(a.astype(BF16), b.astype(BF16), preferred_element_type=F32)


def _dot_nt(a, b):
    return lax.dot_general(a.astype(BF16), b.astype(BF16), (((1,), (1,)), ((), ())),
                           preferred_element_type=F32)


def _dot_tn(a, b):
    return lax.dot_general(a.astype(BF16), b.astype(BF16), (((0,), (0,)), ((), ())),
                           preferred_element_type=F32)


def _split2(x):
    hi = x.astype(BF16)
    lo = (x - hi.astype(F32)).astype(BF16)
    return hi, lo


def _split3(x):
    hi = x.astype(BF16)
    r = x - hi.astype(F32)
    mid = r.astype(BF16)
    lo = (r - mid.astype(F32)).astype(BF16)
    return hi, mid, lo


def _dot_sel_k3(w3, x):
    return jnp.dot(w3, jnp.concatenate(_split3(x), 0), preferred_element_type=F32)


def _dot_sel_r(x, w2):
    hi, mid, lo = _split3(x)
    return (jnp.dot(jnp.concatenate([hi, mid], -1), w2, preferred_element_type=F32)
            + jnp.dot(lo, w2[:x.shape[1]], preferred_element_type=F32))


def _dot_sel_nt(w, x):
    dn = (((1,), (1,)), ((), ()))
    hi, mid, lo = _split3(x)
    return (lax.dot_general(w, hi, dn, preferred_element_type=F32)
            + lax.dot_general(w, mid, dn, preferred_element_type=F32)
            + lax.dot_general(w, lo, dn, preferred_element_type=F32))


def _dot3(a, b):
    ah, al = _split2(a)
    bh, bl = _split2(b)
    return (jnp.dot(ah, bh, preferred_element_type=F32) + jnp.dot(ah, bl, preferred_element_type=F32)
            + jnp.dot(al, bh, preferred_element_type=F32))


def _seg_sum(x, ones_m, exp_m):
    hi, lo = _split2(x)
    s = jnp.dot(hi, ones_m, preferred_element_type=F32) + jnp.dot(lo, ones_m, preferred_element_type=F32)
    return jnp.dot(jnp.concatenate(_split2(s), -1), exp_m, preferred_element_type=F32)


def _layer_norm(x, g, b, eps=LN_EPS):
    mu = jnp.mean(x, -1, keepdims=True)
    xc = x - mu
    var = jnp.mean(xc * xc, -1, keepdims=True)
    return xc * lax.rsqrt(var + eps) * g + b


def _sigmoid(x):
    return 0.5 + 0.5 * jnp.tanh(0.5 * x)


def _softplus(x):
    return jnp.maximum(x, 0.0) + jnp.log1p(jnp.exp(-jnp.abs(x)))


def _iota2(shape, dim):
    return lax.broadcasted_iota(jnp.int32, shape, dim)


def _tile_tri(rows, L):
    i = np.arange(rows)[:, None]
    j = np.arange(rows)[None, :]
    same = (i // L) == (j // L)
    return (same & (j <= i)).astype(np.float32), same.astype(np.float32)


def _level_ref(G, b):
    L, W = G.shape
    span = 2 * b
    if span >= SUBLANES:
        return jnp.concatenate([jnp.broadcast_to(G[s + b - 1:s + b], (span, W)) for s in range(0, L, span)], 0)
    pos = (_iota2((L, 1), 0) & (SUBLANES - 1)) >> int(math.log2(span))
    ref = None
    for j in range(SUBLANES // span):
        cand = jnp.concatenate([jnp.broadcast_to(G[s + j * span + b - 1:s + j * span + b], (SUBLANES, W))
                                for s in range(0, L, SUBLANES)], 0)
        ref = cand if ref is None else jnp.where(pos == j, cand, ref)
    return ref


def _group_mats(width, group):
    i = np.arange(width)[:, None]
    j = np.arange(LANES)[None, :]
    ones = ((i // group) == j).astype(np.float32)
    return ones, np.concatenate([ones.T, ones.T], 0)


def _const(a):
    return jnp.asarray(a, BF16)


def _ffn_kernel(x_ref, wup_ref, wdn_ref, g_ref, b_ref, o_ref):
    x = x_ref[...]
    xb = x.astype(BF16)
    acc = jnp.zeros(x.shape, F32)
    for j in range(FFN_HIDDEN // FFN_TH):
        gate = jnp.dot(xb, wup_ref[:, j * FFN_TH:(j + 1) * FFN_TH], preferred_element_type=F32)
        up = jnp.dot(xb, wup_ref[:, FFN_HIDDEN + j * FFN_TH:FFN_HIDDEN + (j + 1) * FFN_TH],
                     preferred_element_type=F32)
        act = (gate * _sigmoid(gate) * up).astype(BF16)
        acc = acc + jnp.dot(act, wdn_ref[j * FFN_TH:(j + 1) * FFN_TH, :], preferred_element_type=F32)
    o_ref[...] = _layer_norm(ALPHA * x + acc, g_ref[...], b_ref[...])


def _resident(shape):
    nd = len(shape)
    return pl.BlockSpec(shape, lambda *_: (0,) * nd, pipeline_mode=pl.Buffered(1))


def _ffn_layer(x, w_up, w_down, ln_g, ln_b, tm):
    M = x.shape[0]
    return pl.pallas_call(
        _ffn_kernel,
        grid=(M // tm,),
        in_specs=[pl.BlockSpec((tm, D_MODEL), lambda i: (i, 0)),
                  _resident(w_up.shape), _resident(w_down.shape),
                  _resident(ln_g.shape), _resident(ln_b.shape)],
        out_specs=pl.BlockSpec((tm, D_MODEL), lambda i: (i, 0)),
        out_shape=jax.ShapeDtypeStruct((M, D_MODEL), F32),
        compiler_params=pltpu.CompilerParams(dimension_semantics=("arbitrary",),
                                             vmem_limit_bytes=VMEM_LIMIT),
        name="ffn_deepnorm",
    )(x, w_up, w_down, ln_g, ln_b)


def _gla_kernel(x_ref, s0_ref, wqkvr_ref, wgl_ref, wgu_ref, bg_ref, gng_ref, gnb_ref, wo_ref,
                lng_ref, lnb_ref, tri_ref, ones_ref, exp_ref,
                y_ref, s_ref, q_s, k_s, v_s, g_s, o_s, *, bb, tm, L):
    R = bb * tm
    cpt = tm // L
    nlev = int(math.log2(L))

    @pl.when(pl.program_id(1) == 0)
    def _():
        s_ref[...] = s0_ref[...]

    x = x_ref[...].reshape(R, D_MODEL)
    xb = x.astype(BF16)
    proj = jnp.dot(xb, wqkvr_ref[...], preferred_element_type=F32)
    q_s[...] = proj[:, :GLA_KD] * (GLA_DK ** -0.5)
    k_s[...] = proj[:, GLA_KD:2 * GLA_KD]
    v_s[...] = proj[:, 2 * GLA_KD:2 * GLA_KD + GLA_VD]
    r = proj[:, 2 * GLA_KD + GLA_VD:]
    gl = jnp.dot(xb, wgl_ref[...], preferred_element_type=F32)
    pre = _dot(gl, wgu_ref[...]) + bg_ref[...]
    g_s[...] = -_softplus(-pre) / GLA_TAU

    row = _iota2((L, L), 0)
    col = _iota2((L, L), 1)
    eye = row == col
    rowv = _iota2((L, 1), 0)
    pair_masks = []
    odd_rows = []
    for li in range(nlev):
        b = L >> (li + 1)
        pair_masks.append(((row ^ col) < 2 * b) & ((row & b) != 0) & ((col & b) == 0))
        odd_rows.append((rowv & b) != 0)
    tri = tri_ref[...]

    heads = range(GLA_HEADS)
    ks = [slice(h * GLA_DK, (h + 1) * GLA_DK) for h in heads]
    vs = [slice(h * GLA_DV, (h + 1) * GLA_DV) for h in heads]
    n_chunks = bb * cpt
    ca = next(c for c in (4, 2, 1) if n_chunks % c == 0)

    def chunks(i, carry):
        ns = [i * ca + c for c in range(ca)]
        rows = [pl.ds(pl.multiple_of(n * L, L), L) for n in ns]
        qs = [q_s[r, :] for r in rows]
        kk = [k_s[r, :] for r in rows]
        vv = [v_s[r, :] for r in rows]
        gs = [g_s[r, :] for r in rows]
        Gs = [_dot_sel_k3(tri, g) for g in gs]
        qks = []
        for q, k, g, G in zip(qs, kk, gs, Gs):
            qk = []
            for li in range(nlev):
                b = L >> (li + 1)
                if b == 1:
                    e = jnp.where(odd_rows[li], g, 0.0)
                else:
                    ref = _level_ref(G, b)
                    e = jnp.where(odd_rows[li], G - ref, ref - G)
                qk.append((jnp.where(odd_rows[li], q, k) * jnp.exp(e)).astype(BF16))
            qks.append(qk)
        scores = [[[_dot_nt(qk[li][:, ks[h]], qk[li][:, ks[h]]) for li in range(nlev)] for h in heads]
                  for qk in qks]
        upds = [[_dot_tn(v[:, vs[h]], (k * jnp.exp(G[L - 1:L] - G))[:, ks[h]]) for h in heads]
                for v, k, G in zip(vv, kk, Gs)]
        intras = []
        for q, k, v, sc in zip(qs, kk, vv, scores):
            per_head = []
            for h in heads:
                A = jnp.where(eye, jnp.sum(q[:, ks[h]] * k[:, ks[h]], -1, keepdims=True), 0.0)
                for li in range(nlev):
                    A = jnp.where(pair_masks[li], sc[h][li], A)
                per_head.append(_dot(A, v[:, vs[h]]))
            intras.append(per_head)
        for n, r, q, G, intra, upd in zip(ns, rows, qs, Gs, intras, upds):
            b_idx = n // cpt
            qg = q * jnp.exp(G)
            e_last = jnp.exp(G[L - 1:L])
            sts = [s_ref[b_idx, h] for h in heads]
            inter = [_dot_nt(qg[:, ks[h]], sts[h]) for h in heads]
            for h in heads:
                o_s[r, vs[h]] = inter[h] + intra[h]
                s_ref[b_idx, h] = sts[h] * e_last[:, ks[h]] + upd[h]
        return carry

    lax.fori_loop(0, n_chunks // ca, chunks, 0)

    o = o_s[...]
    mu = _seg_sum(o, ones_ref[...], exp_ref[...]) * (1.0 / GLA_DV)
    oc = o - mu
    var = _seg_sum(oc * oc, ones_ref[...], exp_ref[...]) * (1.0 / GLA_DV)
    on = oc * lax.rsqrt(var + LN_EPS) * gng_ref[...] + gnb_ref[...]
    h_mix = _dot(on * (r * _sigmoid(r)), wo_ref[...])
    y_ref[...] = _layer_norm(ALPHA * x + h_mix, lng_ref[...], lnb_ref[...]).reshape(bb, tm, D_MODEL)


def _mixer_call(kernel, name, x, states, consts, out_state_shapes, scratch, bb, tm):
    B, T, _ = x.shape
    grid = (B // bb, T // tm)

    def state_spec(shape):
        nd = len(shape)
        return pl.BlockSpec((bb,) + tuple(shape[1:]), lambda b, t: (b,) + (0,) * (nd - 1))

    in_specs = ([pl.BlockSpec((bb, tm, D_MODEL), lambda b, t: (b, t, 0))]
                + [state_spec(s.shape) for s in states]
                + [_resident(c.shape) for c in consts])
    out_specs = ([pl.BlockSpec((bb, tm, D_MODEL), lambda b, t: (b, t, 0))]
                 + [state_spec(s) for s in out_state_shapes])
    out_shape = ([jax.ShapeDtypeStruct((B, T, D_MODEL), F32)]
                 + [jax.ShapeDtypeStruct(s, F32) for s in out_state_shapes])
    return pl.pallas_call(
        kernel, grid=grid, in_specs=in_specs, out_specs=out_specs, out_shape=out_shape,
        scratch_shapes=scratch,
        compiler_params=pltpu.CompilerParams(dimension_semantics=("arbitrary", "arbitrary"),
                                             vmem_limit_bytes=VMEM_LIMIT),
        name=name,
    )(x, *states, *consts)


def _gla_layer(x, s0, w, ln_g, ln_b, bb, tm, L):
    R = bb * tm
    ones_m, exp_m = _group_mats(GLA_VD, GLA_DV)
    consts = [w["wqkvr"], w["wgl"], w["wgu"], w["bg"], w["gng"], w["gnb"], w["wo"], ln_g, ln_b,
              _const(np.tile(_tile_tri(L, L)[0], (1, 3))), _const(ones_m), _const(exp_m)]
    scratch = [pltpu.VMEM((R, GLA_KD), F32), pltpu.VMEM((R, GLA_KD), F32), pltpu.VMEM((R, GLA_VD), F32),
               pltpu.VMEM((R, GLA_KD), F32), pltpu.VMEM((R, GLA_VD), F32)]
    kern = functools.partial(_gla_kernel, bb=bb, tm=tm, L=L)
    return _mixer_call(kern, "gla_mixer", x, [s0], consts, [s0.shape], scratch, bb, tm)


def _lo_half(shape):
    return _iota2(shape, 1) < shape[1] // 2


def _bd(z):
    lo = _lo_half(z.shape)
    zero = jnp.zeros_like(z)
    return jnp.concatenate([jnp.where(lo, z, zero), jnp.where(lo, zero, z)], 0)


def _dot3_bd_many(ys, z):
    M = ys[0].shape[0]
    n = len(ys)
    parts = [_split2(y) for y in ys]
    his = [h for h, _ in parts]
    zh, zl = _split2(z)
    hh = jnp.dot(jnp.concatenate(his + [l for _, l in parts], 0), _bd(zh), preferred_element_type=F32)
    hl = jnp.dot(jnp.concatenate(his, 0) if n > 1 else his[0], _bd(zl), preferred_element_type=F32)
    return [hh[i * M:(i + 1) * M] + hh[(n + i) * M:(n + i + 1) * M] + hl[i * M:(i + 1) * M] for i in range(n)]


def _dot3_bd(y, z):
    return _dot3_bd_many([y], z)[0]


def _tri_inverse_pairs(Ts, L):
    row = _iota2((L, 2 * L), 0)
    col = _iota2((L, 2 * L), 1) & (L - 1)
    eye = (row == col).astype(F32)
    bs = min(16, L)
    Ps = [jnp.where((row ^ col) < bs, T, 0.0) for T in Ts]
    Xs = [eye - P for P in Ps]
    if bs > 2:
        Ps = [_dot3_bd(P, P) for P in Ps]
    p = 2
    while p < bs:
        if 2 * p < bs:
            both = [_dot3_bd_many([X, P], P) for X, P in zip(Xs, Ps)]
            Xs = [X + xp for X, (xp, _) in zip(Xs, both)]
            Ps = [pp for _, pp in both]
        else:
            Xs = [X + _dot3_bd(X, P) for X, P in zip(Xs, Ps)]
        p *= 2
    s = bs
    while s < L:
        off = ((row ^ col) < 2 * s) & ((row ^ col) >= s)
        XC = [_dot3_bd(X, jnp.where(off, T, 0.0)) for X, T in zip(Xs, Ts)]
        Xs = [X - _dot3_bd(Y, X) for X, Y in zip(Xs, XC)]
        s *= 2
    return Xs


def _rwkv_kernel(x_ref, s0_ref, sh0_ref, mu_ref, wr_ref, wk_ref, wv_ref, wo_ref, w0_ref, w1_ref, w2_ref,
                 a0_ref, a1_ref, a2_ref, g1_ref, g2_ref, kk_ref, ka_ref, rk_ref, lxg_ref, lxb_ref,
                 lng_ref, lnb_ref, tri_ref, ones_ref, exp_ref,
                 y_ref, s_ref, sh_ref, xs_s, pa_s, xr_s, wy_s, yh_s, *, bb, tm, L):
    R = bb * tm
    cpt = tm // L
    H, N = RWKV_HEADS, RWKV_HEAD

    @pl.when(pl.program_id(1) == 0)
    def _():
        s_ref[...] = s0_ref[...]
        xs_s[:, SUBLANES - 1:SUBLANES, :] = sh0_ref[...]

    x3 = x_ref[...]
    xs_s[:, SUBLANES:SUBLANES + tm, :] = x3
    xprev = xs_s[:, SUBLANES - 1:SUBLANES - 1 + tm, :].reshape(R, D_MODEL)
    last = x3[:, tm - 1:tm, :]
    xs_s[:, SUBLANES - 1:SUBLANES, :] = last
    sh_ref[...] = last
    x = x3.reshape(R, D_MODEL)
    xx = xprev - x
    mu = mu_ref[...]

    def mix(c):
        return (x + xx * mu[c:c + 1, :]).astype(BF16)

    r = jnp.dot(mix(0), wr_ref[...], preferred_element_type=F32)
    k = jnp.dot(mix(2), wk_ref[...], preferred_element_type=F32)
    v = jnp.dot(mix(3), wv_ref[...], preferred_element_type=F32)
    wl = jnp.tanh(jnp.dot(mix(1), w1_ref[...], preferred_element_type=F32))
    w_log = -_softplus(-(w0_ref[...] + _dot(wl, w2_ref[...]))) - 0.5
    lw = -jnp.exp(w_log)
    al = jnp.dot(mix(4), a1_ref[...], preferred_element_type=F32)
    a = _sigmoid(a0_ref[...] + _dot(al, a2_ref[...]))
    gg = _dot(_sigmoid(jnp.dot(mix(5), g1_ref[...], preferred_element_type=F32)), g2_ref[...])

    ones_m = ones_ref[...]
    exp_m = exp_ref[...]
    kk = k * kk_ref[...]
    kk = kk * lax.rsqrt(_seg_sum(kk * kk, ones_m, exp_m) + 1e-12)
    k = k * (1.0 + (a - 1.0) * ka_ref[...])
    bvec = kk * a

    tri3 = tri_ref[...]
    gams = [_dot_sel_k3(tri3, lw[c * L:(c + 1) * L]) for c in range(R // L)]
    gam = jnp.concatenate(gams, 0)
    gl = jnp.concatenate([jnp.broadcast_to(gc[L - 1:L], (L, D_MODEL)) for gc in gams], 0)
    e_neg = jnp.exp(-gam)
    e_rem = jnp.exp(gl - gam)
    alpha = kk * jnp.exp(gam - lw)
    rho = r * jnp.exp(gam)
    beta = bvec * e_neg
    kap = k * e_neg
    kap_l = k * e_rem
    bet_l = -(bvec * e_rem)
    e_tot = jnp.exp(gl)
    PW = 2 * N
    NP = H // 2
    for p in range(NP):
        ps = slice(p * PW, (p + 1) * PW)
        for slot, arr in enumerate((alpha, rho, beta, kap, v, kap_l, bet_l, e_tot)):
            pa_s[p, slot] = arr[:, ps]

    rowp = _iota2((L, 2 * L), 0)
    colp = _iota2((L, 2 * L), 1) & (L - 1)
    strict = colp < rowp
    incl = colp <= rowp
    lo_state = _lo_half((N, PW))

    def chunk_rows(n):
        return pl.ds(pl.multiple_of(n * L, L), L)

    n_chunks = bb * cpt
    ca = 2 if n_chunks % 2 == 0 else 1

    def prepare(i, carry):
        items = [(chunk_rows(i * ca + c), p) for c in range(ca) for p in range(NP)]
        ARs = [jnp.concatenate([pa_s[p, 0, rows, :], pa_s[p, 1, rows, :]], 0) for rows, p in items]
        Ms = [_dot_nt(AR, jnp.concatenate([_bd(pa_s[p, 2, rows, :]), _bd(pa_s[p, 3, rows, :])], 0))
              for AR, (rows, p) in zip(ARs, items)]
        vbd = [_bd(pa_s[p, 4, rows, :].astype(BF16)) for rows, p in items]
        Xs = _tri_inverse_pairs([jnp.where(strict, M[:L, :2 * L], 0.0) for M in Ms], L)
        for (rows, p), M, X, vb in zip(items, Ms, Xs, vbd):
            xr_s[p, 0, rows, :] = X
            xr_s[p, 1, rows, :] = jnp.where(incl, M[L:, :2 * L], 0.0)
            wy_s[p, 0, rows, :] = _dot(jnp.where(strict, M[:L, 2 * L:], 0.0), vb)
            wy_s[p, 1, rows, :] = _dot(jnp.where(incl, M[L:, 2 * L:], 0.0), vb)
        return carry

    lax.fori_loop(0, n_chunks // ca, prepare, 0)

    def recur(c, carry):
        items = [(b, chunk_rows(b * cpt + c), p) for b in range(bb) for p in range(NP)]
        Ss = [s_ref[b, p] for b, _, p in items]
        W0s = [_dot_nt(jnp.concatenate([pa_s[p, 0, rows, :], pa_s[p, 1, rows, :]], 0), _bd(S))
               for (_, rows, p), S in zip(items, Ss)]
        Us = [_dot3_bd(xr_s[p, 0, rows, :], W0[:L] + wy_s[p, 0, rows, :])
              for (_, rows, p), W0 in zip(items, W0s)]
        for (b, rows, p), S, W0, U in zip(items, Ss, W0s, Us):
            yh_s[p, rows, :] = W0[L:] - _dot(xr_s[p, 1, rows, :], _bd(U.astype(BF16))) + wy_s[p, 1, rows, :]
            kb = jnp.concatenate([pa_s[p, 5, rows, :], pa_s[p, 6, rows, :]], 0)
            cross = _dot_tn(jnp.concatenate([pa_s[p, 4, rows, :], U], 0), kb)
            e_row = pa_s[p, 7, pl.ds(pl.multiple_of((b * cpt + c) * L, L), 1), :]
            s_ref[b, p] = S * e_row + jnp.where(lo_state, cross[:N], cross[N:])
        return carry

    lax.fori_loop(0, cpt, recur, 0)

    y = jnp.concatenate([yh_s[p] for p in range(NP)], -1)
    mu_y = _seg_sum(y, ones_m, exp_m) * (1.0 / N)
    yc = y - mu_y
    var_y = _seg_sum(yc * yc, ones_m, exp_m) * (1.0 / N)
    yn = yc * lax.rsqrt(var_y + RWKV_LN_EPS) * lxg_ref[...] + lxb_ref[...]
    bonus = _seg_sum(r * k * rk_ref[...], ones_m, exp_m) * v
    h_mix = _dot((yn + bonus) * gg, wo_ref[...])
    y_ref[...] = _layer_norm(ALPHA * x + h_mix, lng_ref[...], lnb_ref[...]).reshape(bb, tm, D_MODEL)


def _rwkv_layer(x, s0, sh0, w, ln_g, ln_b, bb, tm, L):
    R = bb * tm
    H, N = RWKV_HEADS, RWKV_HEAD
    tri = np.tile(_tile_tri(L, L)[0], (1, 3))
    ones_m, exp_m = _group_mats(D_MODEL, N)
    consts = [w["mu"], w["wr"], w["wk"], w["wv"], w["wo"], w["w0"], w["w1"], w["w2"], w["a0"], w["a1"],
              w["a2"], w["g1"], w["g2"], w["kk"], w["ka"], w["rk"], w["lxg"], w["lxb"], ln_g, ln_b,
              _const(tri), _const(ones_m), _const(exp_m)]
    scratch = [pltpu.VMEM((bb, tm + SUBLANES, D_MODEL), F32),
               pltpu.VMEM((H // 2, 8, R, 2 * N), F32), pltpu.VMEM((H // 2, 2, R, 2 * L), F32),
               pltpu.VMEM((H // 2, 2, R, 2 * N), F32), pltpu.VMEM((H // 2, R, 2 * N), F32)]
    kern = functools.partial(_rwkv_kernel, bb=bb, tm=tm, L=L)
    return _mixer_call(kern, "rwkv_mixer", x, [s0, sh0], consts, [s0.shape, sh0.shape], scratch, bb, tm)


def _ssd_kernel(x_ref, s0_ref, cv0_ref, wz_ref, wxbc_ref, wdts_ref, cw_ref, cb_ref,
                dbs_ref, alx_ref, als_ref, dsk_ref, ng_ref, wo_ref, lng_ref, lnb_ref,
                tri_ref, eye_ref, hexp_ref, ones_ref, exp_ref,
                y_ref, s_ref, cv_ref, xs_s, xh_s, xdt_s, b_s, c_s, ax_s, as_s, yo_s, *, bb, tm, L):
    R = bb * tm
    cpt = tm // L
    W = SSD_CONV
    PAD = SUBLANES

    @pl.when(pl.program_id(1) == 0)
    def _():
        s_ref[...] = s0_ref[...]
        xs_s[:, PAD - (W - 1):PAD, :] = cv0_ref[...]

    x = x_ref[...].reshape(R, D_MODEL)
    xb = x.astype(BF16)
    xbc_raw = jnp.dot(xb, wxbc_ref[...], preferred_element_type=F32)
    xs_s[:, PAD:PAD + tm, :] = xbc_raw.reshape(bb, tm, SSD_CONV_DIM)
    cw = cw_ref[...]
    conv = cb_ref[...].reshape(1, 1, SSD_CONV_DIM)
    for wi in range(W):
        lo = PAD - (W - 1) + wi
        conv = conv + xs_s[:, lo:lo + tm, :] * cw[wi:wi + 1, :].reshape(1, 1, SSD_CONV_DIM)
    tail = xs_s[:, PAD + tm - (W - 1):PAD + tm, :]
    xs_s[:, PAD - (W - 1):PAD, :] = tail
    cv_ref[...] = tail
    conv = conv.reshape(R, SSD_CONV_DIM)
    xbc = conv * _sigmoid(conv)
    xh = xbc[:, :SSD_DI]
    dt_s = _softplus(jnp.dot(xb, wdts_ref[...], preferred_element_type=F32) + dbs_ref[...])
    dt_x = _dot_sel_r(dt_s, hexp_ref[...])
    xh_s[...] = xh
    xdt_s[...] = xh * dt_x
    b_s[...] = xbc[:, SSD_DI:SSD_DI + SSD_GN]
    c_s[...] = xbc[:, SSD_DI + SSD_GN:]
    ax_s[...] = dt_x * (-jnp.exp(alx_ref[...]))
    as_s[...] = dt_s * (-jnp.exp(als_ref[...]))

    row = _iota2((L, L), 0)
    col = _iota2((L, L), 1)
    causal = col <= row
    tri = tri_ref[...]
    eye = eye_ref[...]

    groups = range(SSD_GROUPS)
    ns = [slice(g * SSD_DSTATE, (g + 1) * SSD_DSTATE) for g in groups]
    gs = [slice(g * SSD_GW, (g + 1) * SSD_GW) for g in groups]
    n_chunks = bb * cpt
    ca = next(c for c in (4, 2, 1) if n_chunks % c == 0)

    def chunks(i, carry):
        nn = [i * ca + c for c in range(ca)]
        rows = [pl.ds(pl.multiple_of(n * L, L), L) for n in nn]
        cums = [_dot_sel_k3(tri, ax_s[r, :]) for r in rows]
        cum_ts = [_dot_sel_nt(eye, _dot_sel_k3(tri, as_s[r, :])) for r in rows]
        xdts = [xdt_s[r, :] for r in rows]
        Bms = [b_s[r, :] for r in rows]
        Cms = [c_s[r, :] for r in rows]
        CBs = [[_dot_nt(Cm[:, ns[g]], Bm[:, ns[g]]) for g in groups] for Cm, Bm in zip(Cms, Bms)]
        upds = [[_dot_tn(Bm[:, ns[g]], (xdt * jnp.exp(cum[L - 1:L] - cum))[:, gs[g]]) for g in groups]
                for Bm, xdt, cum in zip(Bms, xdts, cums)]
        intras = []
        for cum, cum_t, xdt, CB in zip(cums, cum_ts, xdts, CBs):
            per_group = []
            for g in groups:
                ys = []
                for hh in range(SSD_HPG):
                    h = g * SSD_HPG + hh
                    d = cum[:, h * SSD_HEADDIM:h * SSD_HEADDIM + L] - cum_t[h:h + 1, :]
                    dec = jnp.exp(jnp.where(causal, d, -jnp.inf))
                    ys.append(_dot(CB[g] * dec, xdt[:, h * SSD_HEADDIM:(h + 1) * SSD_HEADDIM]))
                per_group.append(jnp.concatenate(ys, -1))
            intras.append(per_group)
        for n, r, cum, Cm, intra, upd in zip(nn, rows, cums, Cms, intras, upds):
            b_idx = n // cpt
            e_cum = jnp.exp(cum)
            e_last = jnp.exp(cum[L - 1:L])
            sts = [s_ref[b_idx, g] for g in groups]
            inter = [_dot(Cm[:, ns[g]], sts[g]) for g in groups]
            for g in groups:
                yo_s[r, gs[g]] = intra[g] + inter[g] * e_cum[:, gs[g]]
                s_ref[b_idx, g] = sts[g] * e_last[:, gs[g]] + upd[g]
        return carry

    lax.fori_loop(0, n_chunks // ca, chunks, 0)

    z = jnp.dot(xb, wz_ref[...], preferred_element_type=F32)
    y = (yo_s[...] + dsk_ref[...] * xh_s[...]) * (z * _sigmoid(z))
    ms = _seg_sum(y * y, ones_ref[...], exp_ref[...]) * (1.0 / SSD_GW)
    yn = y * lax.rsqrt(ms + LN_EPS) * ng_ref[...]
    h_mix = _dot(yn, wo_ref[...])
    y_ref[...] = _layer_norm(ALPHA * x + h_mix, lng_ref[...], lnb_ref[...]).reshape(bb, tm, D_MODEL)


def _ssd_layer(x, s0, cv0, w, ln_g, ln_b, bb, tm, L):
    R = bb * tm
    tri, _ = _tile_tri(L, L)
    ones_m, exp_m = _group_mats(SSD_DI, SSD_GW)
    head_exp = (np.arange(LANES)[:, None] == np.arange(SSD_DI)[None, :] // SSD_HEADDIM).astype(np.float32)
    consts = [w["wz"], w["wxbc"], w["wdts"], w["cw"], w["cb"], w["dbs"], w["alx"],
              w["als"], w["dsk"], w["ng"], w["wo"], ln_g, ln_b,
              _const(np.tile(tri, (1, 3))), _const(np.eye(LANES, dtype=np.float32)),
              _const(np.tile(head_exp, (2, 1))),
              _const(ones_m), _const(exp_m)]
    scratch = [pltpu.VMEM((bb, tm + SUBLANES, SSD_CONV_DIM), F32),
               pltpu.VMEM((R, SSD_DI), F32), pltpu.VMEM((R, SSD_DI), F32),
               pltpu.VMEM((R, SSD_GN), F32), pltpu.VMEM((R, SSD_GN), F32),
               pltpu.VMEM((R, SSD_DI), F32), pltpu.VMEM((R, LANES), F32), pltpu.VMEM((R, SSD_DI), F32)]
    kern = functools.partial(_ssd_kernel, bb=bb, tm=tm, L=L)
    return _mixer_call(kern, "ssd_mixer", x, [s0, cv0], consts, [s0.shape, cv0.shape], scratch, bb, tm)


def _row(v):
    return v.reshape(1, -1).astype(F32)


def _pad_cols(w, n):
    return jnp.pad(w, ((0, 0), (0, n - w.shape[1])))


def _pad_rows(w, n):
    return jnp.pad(w, ((0, n - w.shape[0]), (0, 0)))


def _prep_gla(p, j):
    w_in = p["gla_w_in"][j]
    o_gl = 2 * GLA_KD + GLA_VD
    return {
        "wqkvr": jnp.concatenate([w_in[:, :o_gl], w_in[:, o_gl + GLA_GATE_RANK:]], 1).astype(BF16),
        "wgl": _pad_cols(w_in[:, o_gl:o_gl + GLA_GATE_RANK], LANES).astype(BF16),
        "wgu": _pad_rows(p["gla_w_gate_up"][j], LANES).astype(BF16),
        "bg": _row(p["gla_b_gate"][j]), "gng": _row(p["gla_gn_g"][j]), "gnb": _row(p["gla_gn_b"][j]),
        "wo": p["gla_w_out"][j].astype(BF16),
    }


def _prep_rwkv(p, j):
    return {
        "mu": p["rwkv_mu"][j].astype(F32),
        "wr": p["rwkv_w_rkv"][j, 0].astype(BF16), "wk": p["rwkv_w_rkv"][j, 1].astype(BF16),
        "wv": p["rwkv_w_rkv"][j, 2].astype(BF16), "wo": p["rwkv_w_o"][j].astype(BF16),
        "w0": _row(p["rwkv_w0"][j]),
        "w1": _pad_cols(p["rwkv_w1"][j], RWKV_LORA_PAD).astype(BF16),
        "w2": _pad_rows(p["rwkv_w2"][j], RWKV_LORA_PAD).astype(BF16),
        "a0": _row(p["rwkv_a0"][j]),
        "a1": _pad_cols(p["rwkv_a1"][j], RWKV_LORA_PAD).astype(BF16),
        "a2": _pad_rows(p["rwkv_a2"][j], RWKV_LORA_PAD).astype(BF16),
        "g1": p["rwkv_g1"][j].astype(BF16), "g2": p["rwkv_g2"][j].astype(BF16),
        "kk": _row(p["rwkv_k_k"][j]), "ka": _row(p["rwkv_k_a"][j]), "rk": _row(p["rwkv_r_k"][j]),
        "lxg": _row(p["rwkv_lnx_g"][j]), "lxb": _row(p["rwkv_lnx_b"][j]),
    }


def _prep_ssd(p, j):
    w_in = p["ssd_w_in"][j]
    w_dt = w_in[:, SSD_DI + SSD_CONV_DIM:]
    rep = lambda v: jnp.repeat(v, SSD_HEADDIM, axis=-1)
    return {
        "wz": w_in[:, :SSD_DI].astype(BF16),
        "wxbc": w_in[:, SSD_DI:SSD_DI + SSD_CONV_DIM].astype(BF16),
        "wdts": _pad_cols(w_dt, LANES).astype(BF16),
        "cw": p["ssd_conv_w"][j].astype(F32), "cb": _row(p["ssd_conv_b"][j]),
        "dbs": _pad_cols(_row(p["ssd_dt_bias"][j]), LANES),
        "alx": _row(rep(p["ssd_A_log"][j])), "als": _pad_cols(_row(p["ssd_A_log"][j]), LANES),
        "dsk": _row(rep(p["ssd_D"][j])), "ng": _row(p["ssd_norm_g"][j]),
        "wo": p["ssd_w_out"][j].astype(BF16),
    }


def _trunk(x, st, W, tiles, L, ffn_tm):
    B, T, _ = x.shape
    new = {"gla": [], "rwkv": [], "shift": [], "ssm": [], "conv": []}
    for i in range(DEPTH):
        j, kind = i // 3, i % 3
        lg, lb = W["ln_g"][i], W["ln_b"][i]
        bb, tm = tiles[kind]
        if kind == 0:
            x, s = _gla_layer(x, st["gla"][j], W["gla"][j], lg[0], lb[0], bb, tm, L)
            new["gla"].append(s)
        elif kind == 1:
            x, s, sh = _rwkv_layer(x, st["rwkv"][j], st["shift"][j], W["rwkv"][j], lg[0], lb[0], bb, tm, L)
            new["rwkv"].append(s)
            new["shift"].append(sh)
        else:
            x, s, c = _ssd_layer(x, st["ssm"][j], st["conv"][j], W["ssd"][j], lg[0], lb[0], bb, tm, L)
            new["ssm"].append(s)
            new["conv"].append(c)
        x = _ffn_layer(x.reshape(B * T, D_MODEL), W["ffn_up"][i], W["ffn_down"][i], lg[1], lb[1],
                       ffn_tm).reshape(B, T, D_MODEL)
    return x, new


def _rwkv_to_kernel(s):
    B = s.shape[0]
    s = s.reshape(B, RWKV_HEADS // 2, 2, RWKV_HEAD, RWKV_HEAD)
    return jnp.swapaxes(s, 2, 3).reshape(B, RWKV_HEADS // 2, RWKV_HEAD, 2 * RWKV_HEAD)


def _rwkv_from_kernel(s):
    B = s.shape[0]
    s = s.reshape(B, RWKV_HEADS // 2, RWKV_HEAD, 2, RWKV_HEAD)
    return jnp.swapaxes(s, 2, 3).reshape(B, RWKV_HEADS, RWKV_HEAD, RWKV_HEAD)


def _ssm_to_kernel(s):
    B = s.shape[0]
    return jnp.swapaxes(s.reshape(B, SSD_GROUPS, SSD_GW, SSD_DSTATE), -1, -2)


def _ssm_from_kernel(s):
    B = s.shape[0]
    return jnp.swapaxes(s, -1, -2).reshape(B, SSD_HEADS, SSD_HEADDIM, SSD_DSTATE)


def kernel(x_prompt, x_sample, state_gla, state_rwkv, state_shift, state_ssm, state_conv, meta_tokens, gla_w_in, gla_w_gate_up, gla_b_gate, gla_gn_g, gla_gn_b, gla_w_out, rwkv_mu, rwkv_w_rkv, rwkv_w_o, rwkv_w0, rwkv_w1, rwkv_w2, rwkv_a0, rwkv_a1, rwkv_a2, rwkv_g1, rwkv_g2, rwkv_k_k, rwkv_k_a, rwkv_r_k, rwkv_lnx_g, rwkv_lnx_b, ssd_w_in, ssd_conv_w, ssd_conv_b, ssd_dt_bias, ssd_A_log, ssd_D, ssd_norm_g, ssd_w_out, ffn_w_up, ffn_w_down, ln_g, ln_b):
    p = dict(gla_w_in=gla_w_in, gla_w_gate_up=gla_w_gate_up, gla_b_gate=gla_b_gate, gla_gn_g=gla_gn_g,
             gla_gn_b=gla_gn_b, gla_w_out=gla_w_out, rwkv_mu=rwkv_mu, rwkv_w_rkv=rwkv_w_rkv,
             rwkv_w_o=rwkv_w_o, rwkv_w0=rwkv_w0, rwkv_w1=rwkv_w1, rwkv_w2=rwkv_w2, rwkv_a0=rwkv_a0,
             rwkv_a1=rwkv_a1, rwkv_a2=rwkv_a2, rwkv_g1=rwkv_g1, rwkv_g2=rwkv_g2, rwkv_k_k=rwkv_k_k,
             rwkv_k_a=rwkv_k_a, rwkv_r_k=rwkv_r_k, rwkv_lnx_g=rwkv_lnx_g, rwkv_lnx_b=rwkv_lnx_b,
             ssd_w_in=ssd_w_in, ssd_conv_w=ssd_conv_w, ssd_conv_b=ssd_conv_b, ssd_dt_bias=ssd_dt_bias,
             ssd_A_log=ssd_A_log, ssd_D=ssd_D, ssd_norm_g=ssd_norm_g, ssd_w_out=ssd_w_out)
    n_gla, n_rwkv, n_ssd = gla_w_in.shape[0], rwkv_mu.shape[0], ssd_w_in.shape[0]
    W = {
        "gla": [_prep_gla(p, j) for j in range(n_gla)],
        "rwkv": [_prep_rwkv(p, j) for j in range(n_rwkv)],
        "ssd": [_prep_ssd(p, j) for j in range(n_ssd)],
        "ffn_up": [ffn_w_up[i].astype(BF16) for i in range(DEPTH)],
        "ffn_down": [ffn_w_down[i].astype(BF16) for i in range(DEPTH)],
        "ln_g": [[_row(ln_g[i, c]) for c in range(2)] for i in range(DEPTH)],
        "ln_b": [[_row(ln_b[i, c]) for c in range(2)] for i in range(DEPTH)],
    }
    Bp, Tp, _ = x_prompt.shape
    Bs, Ts, _ = x_sample.shape
    assert Ts == N_META and Tp % CHUNK == 0

    def with_meta(s):
        return jnp.concatenate([jnp.zeros((Bp,) + s.shape[1:], F32), s.astype(F32)], 0)

    x_small = jnp.concatenate(
        [jnp.broadcast_to(meta_tokens.astype(F32)[None], (Bp, N_META, D_MODEL)), x_sample.astype(F32)], 0)
    st_small = {
        "gla": [with_meta(jnp.swapaxes(state_gla[j], -1, -2)) for j in range(n_gla)],
        "rwkv": [with_meta(_rwkv_to_kernel(state_rwkv[j])) for j in range(n_rwkv)],
        "shift": [with_meta(state_shift[j]) for j in range(n_rwkv)],
        "ssm": [with_meta(_ssm_to_kernel(state_ssm[j])) for j in range(n_ssd)],
        "conv": [with_meta(state_conv[j]) for j in range(n_ssd)],
    }
    Bsm = Bp + Bs
    bb_small = next(c for c in (6, 3, 2, 1) if Bsm % c == 0)
    y_small, new_small = _trunk(x_small, st_small, W, [(bb_small, N_META)] * 3, N_META, Bsm * N_META)
    st_prompt = {k: [s[:Bp] for s in v] for k, v in new_small.items()}
    tm = 256 if Tp % 256 == 0 else CHUNK
    ffn_tm = 512 if (Bp * Tp) % 512 == 0 else CHUNK
    tiles = [(1, tm), (Bp, max(tm // Bp, CHUNK)), (1, tm)]
    y_prompt, new_prompt = _trunk(x_prompt.astype(F32), st_prompt, W, tiles, CHUNK, ffn_tm)

    def outs(new, sl):
        return (jnp.stack([jnp.swapaxes(s[sl], -1, -2) for s in new["gla"]]),
                jnp.stack([_rwkv_from_kernel(s[sl]) for s in new["rwkv"]]),
                jnp.stack([s[sl] for s in new["shift"]]),
                jnp.stack([_ssm_from_kernel(s[sl]) for s in new["ssm"]]),
                jnp.stack([s[sl] for s in new["conv"]]))

    return (y_prompt, y_small[Bp:]) + outs(new_prompt, slice(None)) + outs(new_small, slice(Bp, None))
```
